```python
import jax, jax.numpy as jnp
from jax import lax
import numpy as np

D_MODEL = 2048
BATCH = 4
SEQ = 4096
DEPTH = 1

HEAD_DIM = 128
MOBA_HEADS = 8
MOBA_WIDTH = MOBA_HEADS * HEAD_DIM
MOBA_BLOCK = 256
MOBA_TOPK = 3
MOBA_Q_CHUNK = 32
ROPE_THETA = 10000.0
GLA_HEADS = 4
GLA_VALUE_WIDTH = D_MODEL - MOBA_WIDTH
GLA_KEY_WIDTH = GLA_VALUE_WIDTH // 2
GLA_DK = GLA_KEY_WIDTH // GLA_HEADS
GLA_DV = GLA_VALUE_WIDTH // GLA_HEADS
GLA_GATE_RANK = 16
GLA_GATE_NORMALIZER = 16.0
GLA_CHUNK = 64
IN_SPLITS = [MOBA_WIDTH, 2 * MOBA_WIDTH, 3 * MOBA_WIDTH,
             3 * MOBA_WIDTH + GLA_KEY_WIDTH,
             3 * MOBA_WIDTH + 2 * GLA_KEY_WIDTH,
             3 * MOBA_WIDTH + 2 * GLA_KEY_WIDTH + GLA_VALUE_WIDTH,
             3 * MOBA_WIDTH + 2 * GLA_KEY_WIDTH + 2 * GLA_VALUE_WIDTH]
IN_WIDTH = 3 * MOBA_WIDTH + 2 * GLA_KEY_WIDTH + 2 * GLA_VALUE_WIDTH + GLA_GATE_RANK
N_EXPERTS = 32
TOP_K = 4
EXPERT_FF = D_MODEL
SWIGLU_ALPHA = 1.702
SWIGLU_LIMIT = 7.0
MOE_ROW_BLOCK = 256
NORM_EPS = 1e-5

kernel_name = 'hybrid_moba_gla_moe'


def rms_norm(x, w):
    x32 = x.astype(jnp.float32)
    y = x32 * lax.rsqrt(jnp.mean(x32 * x32, axis=-1, keepdims=True) + NORM_EPS)
    return (y * w.astype(jnp.float32)).astype(x.dtype)


def rope_tables(seq_len):
    inv_freq = 1.0 / (ROPE_THETA ** (jnp.arange(0, HEAD_DIM, 2, dtype=jnp.float32) / HEAD_DIM))
    ang = jnp.arange(seq_len, dtype=jnp.float32)[:, None] * inv_freq[None, :]
    return jnp.cos(ang), jnp.sin(ang)


def apply_rope(t, cos, sin):
    t32 = t.astype(jnp.float32)
    t1, t2 = jnp.split(t32, 2, axis=-1)
    c = cos[None, :, None, :]
    s = sin[None, :, None, :]
    return jnp.concatenate([t1 * c - t2 * s, t2 * c + t1 * s], axis=-1).astype(t.dtype)


def moba_attention(q, k, v):
    B, S, H, hd = q.shape
    nb = -(-S // MOBA_BLOCK)
    s_pad = nb * MOBA_BLOCK
    n_sel = min(MOBA_TOPK, nb)
    scale = hd ** -0.5

    def prep(t):
        t = jnp.swapaxes(t, 1, 2)
        return jnp.pad(t, ((0, 0), (0, 0), (0, s_pad - S), (0, 0)))

    q, k, v = prep(q), prep(k), prep(v)
    kb = k.reshape(B, H, nb, MOBA_BLOCK, hd)
    vb = v.reshape(B, H, nb, MOBA_BLOCK, hd)
    k_mean = jnp.mean(kb.astype(jnp.float32), axis=3)
    gate = jnp.einsum('bhsd,bhnd->bhsn', q.astype(jnp.float32), k_mean)
    q_block = jnp.arange(s_pad) // MOBA_BLOCK
    fully_past = jnp.arange(nb)[None, :] < q_block[:, None]
    gate = jnp.where(fully_past, gate, -jnp.inf)
    _, sel = lax.top_k(gate, n_sel)
    sel_valid = sel < q_block[:, None]

    n_qc = s_pad // MOBA_Q_CHUNK

    def chunks(t):
        return jnp.moveaxis(t.reshape(B, H, n_qc, MOBA_Q_CHUNK, t.shape[-1]), 2, 0)

    b_idx = jnp.arange(B)[:, None, None, None]
    h_idx = jnp.arange(H)[None, :, None, None]

    def attend(args):
        c, qc, sc, sv = args
        q_start = c * MOBA_Q_CHUNK
        q_pos = q_start + jnp.arange(MOBA_Q_CHUNK)
        blk = q_start // MOBA_BLOCK
        k_own = lax.dynamic_index_in_dim(kb, blk, axis=2, keepdims=False)
        v_own = lax.dynamic_index_in_dim(vb, blk, axis=2, keepdims=False)
        k_sel = kb[b_idx, h_idx, sc]
        v_sel = vb[b_idx, h_idx, sc]
        s_sel = jnp.einsum('bhqd,bhqnkd->bhqnk', qc, k_sel).astype(jnp.float32) * scale
        s_sel = jnp.where(sv[..., None], s_sel, -jnp.inf).reshape(B, H, MOBA_Q_CHUNK, n_sel * MOBA_BLOCK)
        s_own = jnp.einsum('bhqd,bhkd->bhqk', qc, k_own).astype(jnp.float32) * scale
        k_pos = blk * MOBA_BLOCK + jnp.arange(MOBA_BLOCK)
        s_own = jnp.where(k_pos[None, :] <= q_pos[:, None], s_own, -jnp.inf)
        p = jax.nn.softmax(jnp.concatenate([s_sel, s_own], axis=-1), axis=-1).astype(v.dtype)
        p_sel = p[..., :n_sel * MOBA_BLOCK].reshape(B, H, MOBA_Q_CHUNK, n_sel, MOBA_BLOCK)
        p_own = p[..., n_sel * MOBA_BLOCK:]
        return (jnp.einsum('bhqnk,bhqnkd->bhqd', p_sel, v_sel)
                + jnp.einsum('bhqk,bhkd->bhqd', p_own, v_own))

    out = lax.map(attend, (jnp.arange(n_qc), chunks(q), chunks(sel), chunks(sel_valid)))
    out = jnp.moveaxis(out, 0, 2).reshape(B, H, s_pad, hd)[:, :, :S]
    return jnp.swapaxes(out, 1, 2).reshape(B, S, H * hd)


def gla_chunked(q, k, v, log_a):
    B, S, H, dk = q.shape
    dv = v.shape[-1]
    nc = S // GLA_CHUNK

    def to_chunks(t):
        return t.astype(jnp.float32).reshape(B, nc, GLA_CHUNK, H, t.shape[-1]).transpose(1, 0, 3, 2, 4)

    causal = jnp.tril(jnp.ones((GLA_CHUNK, GLA_CHUNK), dtype=bool))

    def step(state, inp):
        qc, kc, vc, ac = inp
        b = jnp.cumsum(ac, axis=2)
        diff = b[:, :, :, None, :] - b[:, :, None, :, :]
        decay = jnp.exp(jnp.where(causal[:, :, None], diff, -jnp.inf))
        scores = jnp.einsum('bhtk,bhsk,bhtsk->bhts', qc, kc, decay)
        o = (jnp.einsum('bhts,bhsv->bhtv', scores, vc)
             + jnp.einsum('bhtk,bhkv->bhtv', qc * jnp.exp(b), state))
        b_last = b[:, :, -1:, :]
        state = (state * jnp.exp(b_last)[:, :, 0, :, None]
                 + jnp.einsum('bhsk,bhsv->bhkv', kc * jnp.exp(b_last - b), vc))
        return state, o

    state0 = jnp.zeros((B, H, dk, dv), jnp.float32)
    _, o = lax.scan(step, state0, (to_chunks(q), to_chunks(k), to_chunks(v), to_chunks(log_a)))
    return o.transpose(1, 0, 3, 2, 4).reshape(B, S, H, dv).astype(v.dtype)


def moe_ffn(h, router_w, router_b, w_gate_up, b_gate_up, w_down, b_down):
    T, D = h.shape
    A = T * TOP_K
    G = MOE_ROW_BLOCK
    logits = (h @ router_w + router_b).astype(jnp.float32)
    top_vals, top_idx = lax.top_k(logits, TOP_K)
    gates = jax.nn.softmax(top_vals, axis=-1).astype(h.dtype)
    flat_e = top_idx.reshape(-1).astype(jnp.int32)
    flat_tok = jnp.arange(A, dtype=jnp.int32) // TOP_K
    flat_g = gates.reshape(-1)
    order = jnp.argsort(flat_e)
    se = flat_e[order]
    counts = jnp.bincount(flat_e, length=N_EXPERTS)
    starts = jnp.cumsum(counts) - counts
    pcounts = (counts + G - 1) // G * G
    pends = jnp.cumsum(pcounts)
    pstarts = pends - pcounts
    dest = pstarts[se] + (jnp.arange(A) - starts[se])
    n_blocks = (A + N_EXPERTS * (G - 1)) // G
    P = n_blocks * G
    row_tok = jnp.full((P,), T, jnp.int32).at[dest].set(flat_tok[order])
    row_gate = jnp.zeros((P,), h.dtype).at[dest].set(flat_g[order])
    block_start = jnp.arange(n_blocks) * G
    block_e = jnp.minimum(jnp.sum(block_start[:, None] >= pends[None, :], axis=1), N_EXPERTS - 1)
    h_pad = jnp.concatenate([h, jnp.zeros((1, D), h.dtype)], axis=0)

    def expert_block(args):
        e, toks = args
        xb = h_pad[toks]
        gu = xb @ w_gate_up[e] + b_gate_up[e]
        g, u = gu[:, :EXPERT_FF], gu[:, EXPERT_FF:]
        g = jnp.minimum(g, SWIGLU_LIMIT)
        u = jnp.clip(u, -SWIGLU_LIMIT, SWIGLU_LIMIT)
        act = (u + 1.0) * (g * jax.nn.sigmoid(SWIGLU_ALPHA * g))
        return act @ w_down[e] + b_down[e]

    y_rows = lax.map(expert_block, (block_e, row_tok.reshape(n_blocks, G))).reshape(P, D)
    y = jax.ops.segment_sum(y_rows * row_gate[:, None], row_tok, num_segments=T + 1)
    return y[:T]


def setup_inputs(seed: int = 0) -> dict:
    key = jax.random.key(seed)
    ks = jax.random.split(key, 16)
    f32 = jnp.float32
    L, D, E, F = DEPTH, D_MODEL, N_EXPERTS, EXPERT_FF

    def nrm(k, shape, scale):
        return jax.random.normal(k, shape, f32) * scale

    return {
        'x': nrm(ks[0], (BATCH, SEQ, D), 1.0),
        'attn_norm_w': 1.0 + nrm(ks[1], (L, D), 0.02),
        'w_in': nrm(ks[2], (L, D, IN_WIDTH), D ** -0.5),
        'gla_gate_w2': nrm(ks[3], (L, GLA_GATE_RANK, GLA_KEY_WIDTH), GLA_GATE_RANK ** -0.5),
        'gla_gate_b': nrm(ks[4], (L, GLA_KEY_WIDTH), 0.1),
        'gla_norm_w': 1.0 + nrm(ks[5], (L, GLA_DV), 0.02),
        'w_out': nrm(ks[6], (L, D, D), D ** -0.5),
        'ffn_norm_w': 1.0 + nrm(ks[7], (L, D), 0.02),
        'router_w': nrm(ks[8], (L, D, E), D ** -0.5),
        'router_b': nrm(ks[9], (L, E), 0.01),
        'w_gate_up': nrm(ks[10], (L, E, D, 2 * F), D ** -0.5),
        'b_gate_up': nrm(ks[11], (L, E, 2 * F), 0.01),
        'w_down': nrm(ks[12], (L, E, F, D), F ** -0.5),
        'b_down': nrm(ks[13], (L, E, D), 0.01),
        'final_norm_w': 1.0 + nrm(ks[14], (D,), 0.02),
    }


def reference(x, attn_norm_w, w_in, gla_gate_w2, gla_gate_b, gla_norm_w, w_out,
              ffn_norm_w, router_w, router_b, w_gate_up, b_gate_up, w_down, b_down,
              final_norm_w):
    B, S, D = x.shape
    cos, sin = rope_tables(S)
    for l in range(DEPTH):
        h = rms_norm(x, attn_norm_w[l])
        proj = jnp.einsum('bsd,de->bse', h, w_in[l])
        mq, mk, mv, gq, gk, gv, gg, gz = jnp.split(proj, IN_SPLITS, axis=-1)
        mq = apply_rope(mq.reshape(B, S, MOBA_HEADS, HEAD_DIM), cos, sin)
        mk = apply_rope(mk.reshape(B, S, MOBA_HEADS, HEAD_DIM), cos, sin)
        mv = mv.reshape(B, S, MOBA_HEADS, HEAD_DIM)
        moba_out = moba_attention(mq, mk, mv)

        gate_logits = (gz @ gla_gate_w2[l] + gla_gate_b[l]).astype(jnp.float32)
        log_a = jax.nn.log_sigmoid(gate_logits) / GLA_GATE_NORMALIZER
        gla_o = gla_chunked(
            gq.reshape(B, S, GLA_HEADS, GLA_DK) * (GLA_DK ** -0.5),
            gk.reshape(B, S, GLA_HEADS, GLA_DK),
            gv.reshape(B, S, GLA_HEADS, GLA_DV),
            log_a.reshape(B, S, GLA_HEADS, GLA_DK))
        gla_o = rms_norm(gla_o, gla_norm_w[l]) * jax.nn.silu(gg.reshape(B, S, GLA_HEADS, GLA_DV))
        mixed = jnp.concatenate([moba_out, gla_o.reshape(B, S, GLA_VALUE_WIDTH)], axis=-1)
        x = x + jnp.einsum('bse,ed->bsd', mixed, w_out[l])

        h = rms_norm(x, ffn_norm_w[l])
        y = moe_ffn(h.reshape(B * S, D), router_w[l], router_b[l], w_gate_up[l],
                    b_gate_up[l], w_down[l], b_down[l])
        x = x + y.reshape(B, S, D)
    return rms_norm(x, final_norm_w)
```

```python
import functools

import jax
import jax.numpy as jnp
from jax import lax
from jax.experimental import pallas as pl
from jax.experimental.pallas import tpu as pltpu

F32 = jnp.float32
BF16 = jnp.bfloat16
I32 = jnp.int32
HIGHEST = lax.Precision.HIGHEST

HEAD_DIM = 128
MOBA_HEADS = 8
MOBA_BLOCK = 256
MOBA_TOPK = 3
ROPE_THETA = 10000.0
GLA_HEADS = 4
GLA_DK = 128
GLA_DV = 256
GLA_GATE_RANK = 16
GLA_GATE_NORMALIZER = 16.0
TOP_K = 4
SWIGLU_ALPHA = 1.702
SWIGLU_LIMIT = 7.0
NORM_EPS = 1e-5

LANES = 128
VMEM_LIMIT = 58 * 1024 * 1024
NEG = -1e30

PROJ_TM = 512
PROJ_TN = 1024
GLA_CHUNK = 64
GLA_GROUP = 8
MOE_SUB = 256
MOE_SUP = 2048
MOE_TF = 256
COMB_TM = 256


def _dot(a, b, **kw):
    return jnp.dot(a, b, preferred_element_type=F32, **kw)


def _dot_nt(a, b, **kw):
    return lax.dot_general(a, b, (((1,), (1,)), ((), ())), preferred_element_type=F32, **kw)


def _dot_tn(a, b, **kw):
    return lax.dot_general(a, b, (((0,), (0,)), ((), ())), preferred_element_type=F32, **kw)


def _rms(x, w):
    return x * lax.rsqrt(jnp.mean(x * x, axis=-1, keepdims=True) + NORM_EPS) * w


def _in_proj_kernel(x_ref, nw_ref, w_ref, wz_ref, w2_ref, gb_ref, cos_ref, sin_ref,
                    proj_ref, la_ref, h_scr, *, n_main, n_rope):
    j = pl.program_id(1)

    @pl.when(j == 0)
    def _():
        h_scr[...] = _rms(x_ref[...], nw_ref[...]).astype(BF16)

    @pl.when(j < n_rope)
    def _():
        acc = _dot(h_scr[...], w_ref[...])
        cosf = cos_ref[...]
        sinf = sin_ref[...]
        for hd in range(acc.shape[1] // HEAD_DIM):
            t = acc[:, hd * HEAD_DIM:(hd + 1) * HEAD_DIM]
            r = t * cosf + pltpu.roll(t, HEAD_DIM // 2, axis=1) * sinf
            proj_ref[:, hd * HEAD_DIM:(hd + 1) * HEAD_DIM] = r.astype(BF16)

    @pl.when((j >= n_rope) & (j < n_main))
    def _():
        proj_ref[...] = _dot(h_scr[...], w_ref[...]).astype(BF16)

    @pl.when(j == n_main)
    def _():
        gz = _dot(h_scr[...], wz_ref[...])
        logits = _dot(gz, w2_ref[...], precision=HIGHEST) + gb_ref[...]
        logsig = jnp.minimum(logits, 0.0) - jnp.log1p(jnp.exp(-jnp.abs(logits)))
        la_ref[...] = logsig * (1.0 / GLA_GATE_NORMALIZER)


def _in_proj(x2d, norm_w, w_main, w_gz, w2_pad, gate_b, cosf, sinf, seq):
    T, D = x2d.shape
    n_cols = w_main.shape[1]
    tm, tn = PROJ_TM, PROJ_TN
    n_main = n_cols // tn
    n_rope = (2 * MOBA_HEADS * HEAD_DIM) // tn
    kw = w2_pad.shape[1]
    s_tiles = seq // tm
    last = n_main - 1
    return pl.pallas_call(
        functools.partial(_in_proj_kernel, n_main=n_main, n_rope=n_rope),
        grid=(T // tm, n_main + 1),
        in_specs=[
            pl.BlockSpec((tm, D), lambda i, j: (i, 0)),
            pl.BlockSpec((1, D), lambda i, j: (0, 0)),
            pl.BlockSpec((D, tn), lambda i, j: (0, jnp.minimum(j, last))),
            pl.BlockSpec((D, LANES), lambda i, j: (0, 0)),
            pl.BlockSpec((LANES, kw), lambda i, j: (0, 0)),
            pl.BlockSpec((1, kw), lambda i, j: (0, 0)),
            pl.BlockSpec((tm, HEAD_DIM), lambda i, j: (i % s_tiles, 0)),
            pl.BlockSpec((tm, HEAD_DIM), lambda i, j: (i % s_tiles, 0)),
        ],
        out_specs=[
            pl.BlockSpec((tm, tn), lambda i, j: (i, jnp.minimum(j, last))),
            pl.BlockSpec((tm, kw), lambda i, j: (i, 0)),
        ],
        out_shape=[
            jax.ShapeDtypeStruct((T, n_cols), BF16),
            jax.ShapeDtypeStruct((T, kw), F32),
        ],
        scratch_shapes=[pltpu.VMEM((tm, D), BF16)],
        compiler_params=pltpu.CompilerParams(
            dimension_semantics=("parallel", "arbitrary"), vmem_limit_bytes=VMEM_LIMIT),
        name="in_proj",
    )(x2d, norm_w, w_main, w_gz, w2_pad, gate_b, cosf, sinf)


def _moba_kernel(q_ref, k_ref, v_ref, o_ref, kmean_scr, vt_scr, mask_scr, *, nb, blk, scale):
    i = pl.program_id(2)

    @pl.when(i == 0)
    def _():
        for n in range(nb):
            kb = k_ref[n * blk:(n + 1) * blk, :].astype(F32)
            kmean_scr[n:n + 1, :] = jnp.sum(kb, axis=0, keepdims=True) * (1.0 / blk)
            vt_scr[n] = v_ref[n * blk:(n + 1) * blk, :].astype(F32).T.astype(BF16)

    q = q_ref[...]
    qf = q.astype(F32)
    gate = _dot_nt(kmean_scr[...], qf, precision=HIGHEST)
    n_iota = lax.broadcasted_iota(I32, (nb, blk), 0)
    rank = jnp.zeros((nb, blk), I32)
    for m in range(nb):
        gm = gate[m:m + 1, :]
        beats = (gm > gate) | ((gm == gate) & (m < n_iota))
        rank = rank + jnp.where(beats & (m < i), 1, 0)
    sel = (n_iota < i) & (rank < MOBA_TOPK)
    mask_scr[...] = jnp.where(sel, 0.0, NEG)

    qs = (qf * scale).astype(BF16)

    def attend(kj, vtj, add_mask, carry):
        m, l, acc = carry
        s = _dot_nt(kj, qs) + add_mask
        m_new = jnp.maximum(m, jnp.max(s, axis=0, keepdims=True))
        alpha = jnp.exp(m - m_new)
        p = jnp.exp(s - m_new)
        l = alpha * l + jnp.sum(p, axis=0, keepdims=True)
        acc = alpha * acc + _dot(vtj, p.astype(BF16))
        return m_new, l, acc

    own = pl.multiple_of(i * blk, blk)
    kpos = lax.broadcasted_iota(I32, (blk, blk), 0)
    qpos = lax.broadcasted_iota(I32, (blk, blk), 1)
    causal = jnp.where(kpos <= qpos, 0.0, NEG)
    carry = (jnp.full((1, blk), NEG, F32), jnp.zeros((1, blk), F32),
             jnp.zeros((q.shape[1], blk), F32))
    carry = attend(k_ref[pl.ds(own, blk), :], vt_scr[i], causal, carry)

    def body(jb, carry):
        off = pl.multiple_of(jb * blk, blk)
        return attend(k_ref[pl.ds(off, blk), :], vt_scr[jb], mask_scr[pl.ds(jb, 1), :], carry)

    m, l, acc = lax.fori_loop(0, i, body, carry)
    o_ref[...] = (acc / l).T.astype(BF16)


def _moba(proj, batch, seq):
    T = proj.shape[0]
    blk = MOBA_BLOCK
    nb = seq // blk
    H = MOBA_HEADS
    return pl.pallas_call(
        functools.partial(_moba_kernel, nb=nb, blk=blk, scale=HEAD_DIM ** -0.5),
        grid=(batch, H, nb),
        in_specs=[
            pl.BlockSpec((blk, HEAD_DIM), lambda b, h, i: (b * nb + i, h)),
            pl.BlockSpec((seq, HEAD_DIM), lambda b, h, i: (b, H + h)),
            pl.BlockSpec((seq, HEAD_DIM), lambda b, h, i: (b, 2 * H + h)),
        ],
        out_specs=pl.BlockSpec((blk, HEAD_DIM), lambda b, h, i: (b * nb + i, h)),
        out_shape=jax.ShapeDtypeStruct((T, H * HEAD_DIM), BF16),
        scratch_shapes=[
            pltpu.VMEM((nb, HEAD_DIM), F32),
            pltpu.VMEM((nb, HEAD_DIM, blk), BF16),
            pltpu.VMEM((nb, blk), F32),
        ],
        compiler_params=pltpu.CompilerParams(
            dimension_semantics=("parallel", "parallel", "arbitrary"), vmem_limit_bytes=VMEM_LIMIT),
        name="moba",
    )(proj, proj, proj)


def _gla_levels(c):
    out = []
    m = c // 2
    while m >= 1:
        out.append(m)
        m //= 2
    return out


def _gla_kernel(q_ref, k_ref, v_ref, g_ref, la_ref, nw_ref, o_ref, st_scr, *, chunk, group, scale):
    C = chunk
    levels = _gla_levels(C)

    @pl.when(pl.program_id(2) == 0)
    def _():
        st_scr[...] = jnp.zeros_like(st_scr)

    t_io = lax.broadcasted_iota(I32, (C, C), 0)
    s_io = lax.broadcasted_iota(I32, (C, C), 1)
    sel_rows = [jnp.where(s_io <= t_io, 1.0, 0.0)]
    masks = []
    for m in levels:
        blk_t = t_io // (2 * m)
        bound = blk_t * (2 * m) + (m - 1)
        sel_rows.append(jnp.where(s_io <= bound, 1.0, 0.0))
        masks.append((blk_t == s_io // (2 * m)) & (t_io % (2 * m) >= m) & (s_io % (2 * m) < m))
    cum_mat = jnp.concatenate(sel_rows, axis=0)
    diag = t_io == s_io
    nw = nw_ref[...]

    for c in range(group):
        rows = slice(c * C, (c + 1) * C)
        la = la_ref[rows, :]
        cums = _dot(cum_mat, la, precision=HIGHEST)
        b = cums[:C]
        q = q_ref[rows, :].astype(F32) * scale
        k = k_ref[rows, :].astype(F32)
        v = v_ref[rows, :]
        a = jnp.where(diag, _dot_nt(q.astype(BF16), k.astype(BF16)), 0.0)
        for li in range(len(levels)):
            e = jnp.exp(-jnp.abs(b - cums[(li + 1) * C:(li + 2) * C]))
            a = a + jnp.where(masks[li], _dot_nt((q * e).astype(BF16), (k * e).astype(BF16)), 0.0)
        st = st_scr[...]
        o = _dot(a.astype(BF16), v) + _dot_nt((q * jnp.exp(b)).astype(BF16), st.astype(BF16))
        b_last = b[C - 1:C, :]
        ke = (k * jnp.exp(b_last - b)).astype(BF16)
        st_scr[...] = st * jnp.exp(b_last) + _dot_tn(v, ke)
        gg = g_ref[rows, :].astype(F32)
        o_ref[rows, :] = (_rms(o, nw) * (gg * jax.nn.sigmoid(gg))).astype(BF16)


def _gla(proj, la, norm_w, batch, seq):
    T = proj.shape[0]
    Hg = GLA_HEADS
    rows = GLA_CHUNK * GLA_GROUP
    ng = seq // rows
    q0 = (3 * MOBA_HEADS * HEAD_DIM) // GLA_DK
    k0 = q0 + Hg
    v0 = (3 * MOBA_HEADS * HEAD_DIM + 2 * Hg * GLA_DK) // GLA_DV
    g0 = v0 + Hg
    return pl.pallas_call(
        functools.partial(_gla_kernel, chunk=GLA_CHUNK, group=GLA_GROUP, scale=GLA_DK ** -0.5),
        grid=(batch, Hg, ng),
        in_specs=[
            pl.BlockSpec((rows, GLA_DK), lambda b, h, g: (b * ng + g, q0 + h)),
            pl.BlockSpec((rows, GLA_DK), lambda b, h, g: (b * ng + g, k0 + h)),
            pl.BlockSpec((rows, GLA_DV), lambda b, h, g: (b * ng + g, v0 + h)),
            pl.BlockSpec((rows, GLA_DV), lambda b, h, g: (b * ng + g, g0 + h)),
            pl.BlockSpec((rows, GLA_DK), lambda b, h, g: (b * ng + g, h)),
            pl.BlockSpec((1, GLA_DV), lambda b, h, g: (0, 0)),
        ],
        out_specs=pl.BlockSpec((rows, GLA_DV), lambda b, h, g: (b * ng + g, h)),
        out_shape=jax.ShapeDtypeStruct((T, Hg * GLA_DV), BF16),
        scratch_shapes=[pltpu.VMEM((GLA_DV, GLA_DK), F32)],
        compiler_params=pltpu.CompilerParams(
            dimension_semantics=("parallel", "parallel", "arbitrary"), vmem_limit_bytes=VMEM_LIMIT),
        name="gla",
    )(proj, proj, proj, proj, la, norm_w)


def _out_proj_kernel(mo_ref, gl_ref, x_ref, wo_ref, nw_ref, rw_ref, rb_ref,
                     x1_ref, h2_ref, idx_ref, gate_ref):
    wm = mo_ref.shape[1]
    x1 = x_ref[...] + _dot(mo_ref[...], wo_ref[:wm, :]) + _dot(gl_ref[...], wo_ref[wm:, :])
    x1_ref[...] = x1
    h2 = _rms(x1, nw_ref[...])
    h2_ref[...] = h2
    logits = _dot(h2, rw_ref[...], precision=HIGHEST) + rb_ref[...]
    lane = lax.broadcasted_iota(I32, logits.shape, 1)
    vals, idxs = [], []
    for _ in range(TOP_K):
        mx = jnp.max(logits, axis=-1, keepdims=True)
        ix = jnp.min(jnp.where(logits == mx, lane, LANES), axis=-1, keepdims=True)
        vals.append(mx)
        idxs.append(ix)
        logits = jnp.where(lane == ix, -jnp.inf, logits)
    exps = [jnp.exp(v - vals[0]) for v in vals]
    denom = exps[0]
    for e in exps[1:]:
        denom = denom + e
    idx_out = jnp.zeros(logits.shape, I32)
    gate_out = jnp.zeros(logits.shape, F32)
    for r in range(TOP_K):
        idx_out = jnp.where(lane == r, idxs[r], idx_out)
        gate_out = jnp.where(lane == r, exps[r] / denom, gate_out)
    idx_ref[...] = idx_out
    gate_ref[...] = gate_out


def _out_proj(moba_o, gla_o, x2d, w_out, norm_w, rw_pad, rb_pad):
    T, D = x2d.shape
    tm = PROJ_TM
    wm, wg = moba_o.shape[1], gla_o.shape[1]
    return pl.pallas_call(
        _out_proj_kernel,
        grid=(T // tm,),
        in_specs=[
            pl.BlockSpec((tm, wm), lambda i: (i, 0)),
            pl.BlockSpec((tm, wg), lambda i: (i, 0)),
            pl.BlockSpec((tm, D), lambda i: (i, 0)),
            pl.BlockSpec((wm + wg, D), lambda i: (0, 0)),
            pl.BlockSpec((1, D), lambda i: (0, 0)),
            pl.BlockSpec((D, LANES), lambda i: (0, 0)),
            pl.BlockSpec((1, LANES), lambda i: (0, 0)),
        ],
        out_specs=[
            pl.BlockSpec((tm, D), lambda i: (i, 0)),
            pl.BlockSpec((tm, D), lambda i: (i, 0)),
            pl.BlockSpec((tm, LANES), lambda i: (i, 0)),
            pl.BlockSpec((tm, LANES), lambda i: (i, 0)),
        ],
        out_shape=[
            jax.ShapeDtypeStruct((T, D), F32),
            jax.ShapeDtypeStruct((T, D), F32),
            jax.ShapeDtypeStruct((T, LANES), I32),
            jax.ShapeDtypeStruct((T, LANES), F32),
        ],
        compiler_params=pltpu.CompilerParams(
            dimension_semantics=("parallel",), vmem_limit_bytes=VMEM_LIMIT),
        name="out_proj",
    )(moba_o, gla_o, x2d, w_out, norm_w, rw_pad, rb_pad)


def _moe_row_copy(h_hbm, xbuf, sem, tok, r):
    return pltpu.make_async_copy(h_hbm.at[pl.ds(tok, 1), :], xbuf.at[pl.ds(r, 1), :], sem)


def _moe_out_copy(acc, y_hbm, sem, r, blk):
    src = acc.at[pl.ds(pl.multiple_of(r * MOE_SUB, MOE_SUB), MOE_SUB), :]
    dst = y_hbm.at[pl.ds(pl.multiple_of(blk * MOE_SUB, MOE_SUB), MOE_SUB), :]
    return pltpu.make_async_copy(src, dst, sem)


def _moe_kernel(se_ref, rs_ref, ns_ref, tok_ref,
                h_hbm, wg_ref, wu_ref, wd_ref, bg_ref, bu_ref, bd_ref, y_hbm,
                xbuf, acc, wgb, wub, wdb, gsem, osem):
    s = pl.program_id(0)
    j = pl.program_id(1)
    nj = pl.num_programs(1)
    nsub = ns_ref[s]
    blk0 = rs_ref[s]

    @pl.when(nsub > 0)
    def _():
        @pl.when(j == 0)
        def _():
            nrows = nsub * MOE_SUB
            base = blk0 * MOE_SUB

            def issue(r, c):
                _moe_row_copy(h_hbm, xbuf, gsem, tok_ref[base + r], r).start()
                return c

            lax.fori_loop(0, nrows, issue, 0)

            def wait(r, c):
                _moe_row_copy(h_hbm, xbuf, gsem, 0, r).wait()
                return c

            lax.fori_loop(0, nrows, wait, 0)

        wgb[...] = wg_ref[0].astype(BF16)
        wub[...] = wu_ref[0].astype(BF16)
        wdb[...] = wd_ref[0].astype(BF16)
        bg = bg_ref[0]
        bu = bu_ref[0]
        bd = bd_ref[0]

        def sub(r, c):
            off = pl.multiple_of(r * MOE_SUB, MOE_SUB)
            xs = xbuf[pl.ds(off, MOE_SUB), :].astype(BF16)
            g = jnp.minimum(_dot(xs, wgb[...]) + bg, SWIGLU_LIMIT)
            u = jnp.clip(_dot(xs, wub[...]) + bu, -SWIGLU_LIMIT, SWIGLU_LIMIT)
            act = (u + 1.0) * (g * jax.nn.sigmoid(SWIGLU_ALPHA * g))
            part = _dot(act.astype(BF16), wdb[...])

            @pl.when(j == 0)
            def _():
                acc[pl.ds(off, MOE_SUB), :] = part + bd

            @pl.when(j > 0)
            def _():
                acc[pl.ds(off, MOE_SUB), :] += part

            return c

        lax.fori_loop(0, nsub, sub, 0)

        @pl.when(j == nj - 1)
        def _():
            def put(r, c):
                _moe_out_copy(acc, y_hbm, osem, r, blk0 + r).start()
                return c

            lax.fori_loop(0, nsub, put, 0)

            def done(r, c):
                _moe_out_copy(acc, y_hbm, osem, r, blk0 + r).wait()
                return c

            lax.fori_loop(0, nsub, done, 0)

    @pl.when((s == pl.num_programs(0) - 1) & (j == nj - 1))
    def _():
        tail0 = rs_ref[pl.num_programs(0)]
        n_blk = y_hbm.shape[0] // MOE_SUB
        acc[0:MOE_SUB, :] = jnp.zeros((MOE_SUB, acc.shape[1]), F32)

        def put(b, c):
            _moe_out_copy(acc, y_hbm, osem, 0, b).start()
            return c

        lax.fori_loop(tail0, n_blk, put, 0)

        def done(b, c):
            _moe_out_copy(acc, y_hbm, osem, 0, b).wait()
            return c

        lax.fori_loop(tail0, n_blk, done, 0)


def _moe(sup_e, sup_blk, sup_nsub, row_tok, h2, w_gu, b_gu, w_dn, b_dn, n_rows):
    T, D = h2.shape
    E, _, F2 = w_gu.shape
    F = F2 // 2
    tf = MOE_TF
    nf = F // tf
    n_sup = sup_e.shape[0]

    def wmap(col0):
        def f(s, j, se, rs, ns, tok):
            return (se[s], 0, col0 + jnp.where(ns[s] > 0, j, nf - 1))
        return f

    def dmap(s, j, se, rs, ns, tok):
        return (se[s], jnp.where(ns[s] > 0, j, nf - 1), 0)

    def bdmap(s, j, se, rs, ns, tok):
        return (se[s], 0, 0)

    grid_spec = pltpu.PrefetchScalarGridSpec(
        num_scalar_prefetch=4,
        grid=(n_sup, nf),
        in_specs=[
            pl.BlockSpec(memory_space=pl.ANY),
            pl.BlockSpec((1, D, tf), wmap(0)),
            pl.BlockSpec((1, D, tf), wmap(nf)),
            pl.BlockSpec((1, tf, D), dmap),
            pl.BlockSpec((1, 1, tf), wmap(0)),
            pl.BlockSpec((1, 1, tf), wmap(nf)),
            pl.BlockSpec((1, 1, D), bdmap),
        ],
        out_specs=pl.BlockSpec(memory_space=pl.ANY),
        scratch_shapes=[
            pltpu.VMEM((MOE_SUP, D), F32),
            pltpu.VMEM((MOE_SUP, D), F32),
            pltpu.VMEM((D, tf), BF16),
            pltpu.VMEM((D, tf), BF16),
            pltpu.VMEM((tf, D), BF16),
            pltpu.SemaphoreType.DMA,
            pltpu.SemaphoreType.DMA,
        ],
    )
    return pl.pallas_call(
        _moe_kernel,
        grid_spec=grid_spec,
        out_shape=jax.ShapeDtypeStruct((n_rows, D), F32),
        compiler_params=pltpu.CompilerParams(
            dimension_semantics=("arbitrary", "arbitrary"), vmem_limit_bytes=VMEM_LIMIT),
        name="moe",
    )(sup_e, sup_blk, sup_nsub, row_tok, h2, w_gu, w_gu, w_dn,
      b_gu.reshape(E, 1, F2), b_gu.reshape(E, 1, F2), b_dn.reshape(E, 1, D))


def _comb_copy(y_hbm, buf, sem, row, kk, r):
    return pltpu.make_async_copy(y_hbm.at[pl.ds(row, 1), :], buf.at[kk, pl.ds(r, 1), :], sem)


def _combine_kernel(dest_ref, y_hbm, x1_ref, gate_ref, nw_ref, o_ref, buf, sem):
    i = pl.program_id(0)
    tm = x1_ref.shape[0]
    base = i * (tm * TOP_K)

    def issue(r, c):
        for kk in range(TOP_K):
            _comb_copy(y_hbm, buf, sem, dest_ref[base + r * TOP_K + kk], kk, r).start()
        return c

    lax.fori_loop(0, tm, issue, 0)

    def wait(r, c):
        for kk in range(TOP_K):
            _comb_copy(y_hbm, buf, sem, 0, kk, r).wait()
        return c

    lax.fori_loop(0, tm, wait, 0)

    gates = gate_ref[...]
    x2 = x1_ref[...]
    for kk in range(TOP_K):
        x2 = x2 + gates[:, kk:kk + 1] * buf[kk]
    o_ref[...] = _rms(x2, nw_ref[...])


def _combine(dest, y_rows, x1, gates, norm_w):
    T, D = x1.shape
    tm = COMB_TM
    grid_spec = pltpu.PrefetchScalarGridSpec(
        num_scalar_prefetch=1,
        grid=(T // tm,),
        in_specs=[
            pl.BlockSpec(memory_space=pl.ANY),
            pl.BlockSpec((tm, D), lambda i, d: (i, 0)),
            pl.BlockSpec((tm, LANES), lambda i, d: (i, 0)),
            pl.BlockSpec((1, D), lambda i, d: (0, 0)),
        ],
        out_specs=pl.BlockSpec((tm, D), lambda i, d: (i, 0)),
        scratch_shapes=[pltpu.VMEM((TOP_K, tm, D), F32), pltpu.SemaphoreType.DMA],
    )
    return pl.pallas_call(
        _combine_kernel,
        grid_spec=grid_spec,
        out_shape=jax.ShapeDtypeStruct((T, D), F32),
        compiler_params=pltpu.CompilerParams(
            dimension_semantics=("arbitrary",), vmem_limit_bytes=VMEM_LIMIT),
        name="combine",
    )(dest, y_rows, x1, gates, norm_w)


def _route(top_idx, n_experts):
    T = top_idx.shape[0]
    A = T * TOP_K
    G, SUP = MOE_SUB, MOE_SUP
    flat_e = top_idx.reshape(A)
    onehot = (flat_e[:, None] == jnp.arange(n_experts, dtype=I32)[None, :]).astype(I32)
    csum = jnp.cumsum(onehot, axis=0)
    counts = csum[-1]
    rank = jnp.sum((csum - 1) * onehot, axis=1)
    pcounts = (counts + G - 1) // G * G
    pends = jnp.cumsum(pcounts)
    pstarts = pends - pcounts
    dest = (pstarts[flat_e] + rank).astype(I32)
    n_rows = (A + n_experts * (G - 1)) // G * G
    row_tok = jnp.zeros((n_rows,), I32).at[dest].set(jnp.arange(A, dtype=I32) // TOP_K)

    n_sup = (A + n_experts * (G - 1) + n_experts * (SUP - G)) // SUP
    nsup_e = (pcounts + SUP - 1) // SUP
    sup_ends = jnp.cumsum(nsup_e)
    total = sup_ends[-1]
    s_ids = jnp.arange(n_sup, dtype=I32)
    s_eff = jnp.minimum(s_ids, total - 1)
    e_of = jnp.sum((s_eff[:, None] >= sup_ends[None, :]).astype(I32), axis=1)
    local = s_eff - (sup_ends - nsup_e)[e_of]
    active = s_ids < total
    blk0 = (pstarts[e_of] + local * SUP) // G
    nsub = jnp.minimum(SUP // G, (pcounts[e_of] - local * SUP) // G)
    sup_e = e_of.astype(I32)
    sup_blk = jnp.concatenate([jnp.where(active, blk0, 0), pends[-1:] // G]).astype(I32)
    sup_nsub = jnp.where(active, nsub, 0).astype(I32)
    return dest, row_tok, n_rows, sup_e, sup_blk, sup_nsub


def _rope_tables(seq):
    half = HEAD_DIM // 2
    inv_freq = 1.0 / (ROPE_THETA ** (jnp.arange(0, HEAD_DIM, 2, dtype=F32) / HEAD_DIM))
    ang = jnp.arange(seq, dtype=F32)[:, None] * inv_freq[None, :]
    cos, sin = jnp.cos(ang), jnp.sin(ang)
    assert cos.shape[1] == half
    return jnp.concatenate([cos, cos], axis=1), jnp.concatenate([-sin, sin], axis=1)


def kernel(x, attn_norm_w, w_in, gla_gate_w2, gla_gate_b, gla_norm_w, w_out, ffn_norm_w,
           router_w, router_b, w_gate_up, b_gate_up, w_down, b_down, final_norm_w):
    B, S, D = x.shape
    T = B * S
    assert w_in.shape[0] == 1, "single-layer stack: the final norm is fused into combine"
    n_experts = router_w.shape[-1]
    n_main = w_in.shape[-1] - GLA_GATE_RANK
    cosf, sinf = _rope_tables(S)
    x2d = x.reshape(T, D)
    w_main = w_in[0, :, :n_main].astype(BF16)
    w_gz = jnp.pad(w_in[0, :, n_main:], ((0, 0), (0, LANES - GLA_GATE_RANK))).astype(BF16)
    w2_pad = jnp.pad(gla_gate_w2[0], ((0, LANES - GLA_GATE_RANK), (0, 0)))
    proj, log_a = _in_proj(x2d, attn_norm_w[0][None, :], w_main, w_gz, w2_pad,
                           gla_gate_b[0][None, :], cosf, sinf, S)
    moba_o = _moba(proj, B, S)
    gla_o = _gla(proj, log_a, gla_norm_w[0][None, :], B, S)
    rw_pad = jnp.pad(router_w[0], ((0, 0), (0, LANES - n_experts)))
    rb_pad = jnp.pad(router_b[0], (0, LANES - n_experts), constant_values=NEG)[None, :]
    x1, h2, top_idx, gates = _out_proj(moba_o, gla_o, x2d, w_out[0].astype(BF16),
                                       ffn_norm_w[0][None, :], rw_pad, rb_pad)
    dest, row_tok, n_rows, sup_e, sup_blk, sup_nsub = _route(top_idx[:, :TOP_K], n_experts)
    y_rows = _moe(sup_e, sup_blk, sup_nsub, row_tok, h2, w_gate_up[0], b_gate_up[0],
                  w_down[0], b_down[0], n_rows)
    out = _combine(dest, y_rows, x1, gates, final_norm_w[None, :])
    return out.reshape(B, S, D)
```

```python
import functools

import jax
import jax.numpy as jnp
from jax import lax
from jax.experimental import pallas as pl
from jax.experimental.pallas import tpu as pltpu

F32 = jnp.float32
BF16 = jnp.bfloat16
I32 = jnp.int32
HIGHEST = lax.Precision.HIGHEST

HEAD_DIM = 128
MOBA_HEADS = 8
MOBA_BLOCK = 256
MOBA_TOPK = 3
ROPE_THETA = 10000.0
GLA_HEADS = 4
GLA_DK = 128
GLA_DV = 256
GLA_GATE_RANK = 16
GLA_GATE_NORMALIZER = 16.0
TOP_K = 4
SWIGLU_ALPHA = 1.702
SWIGLU_LIMIT = 7.0
NORM_EPS = 1e-5

LANES = 128
VMEM_LIMIT = 58 * 1024 * 1024
NEG = -1e30

PROJ_TM = 512
PROJ_TN = 1024
GLA_CHUNK = 64
GLA_GROUP = 8
MOE_SUB = 256
MOE_SUP = 2048
MOE_TF = 256
MOE_DMA_UNROLL = 8
COMB_TM = 256
COMB_UNROLL = 2


def _dot(a, b, **kw):
    return jnp.dot(a, b, preferred_element_type=F32, **kw)


def _dot_nt(a, b, **kw):
    return lax.dot_general(a, b, (((1,), (1,)), ((), ())), preferred_element_type=F32, **kw)


def _dot_tn(a, b, **kw):
    return lax.dot_general(a, b, (((0,), (0,)), ((), ())), preferred_element_type=F32, **kw)


def _rms(x, w):
    return x * lax.rsqrt(jnp.mean(x * x, axis=-1, keepdims=True) + NORM_EPS) * w


def _in_proj_kernel(x_ref, nw_ref, w_ref, wz_ref, w2_ref, gb_ref, cos_ref, sin_ref,
                    proj_ref, la_ref, h_scr, *, n_main, n_rope):
    j = pl.program_id(1)

    @pl.when(j == 0)
    def _():
        h_scr[...] = _rms(x_ref[...], nw_ref[...]).astype(BF16)

    @pl.when(j < n_rope)
    def _():
        acc = _dot(h_scr[...], w_ref[...])
        cosf = cos_ref[...]
        sinf = sin_ref[...]
        for hd in range(acc.shape[1] // HEAD_DIM):
            t = acc[:, hd * HEAD_DIM:(hd + 1) * HEAD_DIM]
            r = t * cosf + pltpu.roll(t, HEAD_DIM // 2, axis=1) * sinf
            proj_ref[:, hd * HEAD_DIM:(hd + 1) * HEAD_DIM] = r.astype(BF16)

    @pl.when((j >= n_rope) & (j < n_main))
    def _():
        proj_ref[...] = _dot(h_scr[...], w_ref[...]).astype(BF16)

    @pl.when(j == n_main)
    def _():
        gz = _dot(h_scr[...], wz_ref[...])
        logits = _dot(gz, w2_ref[...], precision=HIGHEST) + gb_ref[...]
        logsig = jnp.minimum(logits, 0.0) - jnp.log1p(jnp.exp(-jnp.abs(logits)))
        la_ref[...] = logsig * (1.0 / GLA_GATE_NORMALIZER)


def _in_proj(x2d, norm_w, w_main, w_gz, w2_pad, gate_b, cosf, sinf, seq):
    T, D = x2d.shape
    n_cols = w_main.shape[1]
    tm, tn = PROJ_TM, PROJ_TN
    n_main = n_cols // tn
    n_rope = (2 * MOBA_HEADS * HEAD_DIM) // tn
    kw = w2_pad.shape[1]
    s_tiles = seq // tm
    last = n_main - 1
    return pl.pallas_call(
        functools.partial(_in_proj_kernel, n_main=n_main, n_rope=n_rope),
        grid=(T // tm, n_main + 1),
        in_specs=[
            pl.BlockSpec((tm, D), lambda i, j: (i, 0)),
            pl.BlockSpec((1, D), lambda i, j: (0, 0)),
            pl.BlockSpec((D, tn), lambda i, j: (0, jnp.minimum(j, last))),
            pl.BlockSpec((D, LANES), lambda i, j: (0, 0)),
            pl.BlockSpec((LANES, kw), lambda i, j: (0, 0)),
            pl.BlockSpec((1, kw), lambda i, j: (0, 0)),
            pl.BlockSpec((tm, HEAD_DIM), lambda i, j: (i % s_tiles, 0)),
            pl.BlockSpec((tm, HEAD_DIM), lambda i, j: (i % s_tiles, 0)),
        ],
        out_specs=[
            pl.BlockSpec((tm, tn), lambda i, j: (i, jnp.minimum(j, last))),
            pl.BlockSpec((tm, kw), lambda i, j: (i, 0)),
        ],
        out_shape=[
            jax.ShapeDtypeStruct((T, n_cols), BF16),
            jax.ShapeDtypeStruct((T, kw), F32),
        ],
        scratch_shapes=[pltpu.VMEM((tm, D), BF16)],
        compiler_params=pltpu.CompilerParams(
            dimension_semantics=("parallel", "arbitrary"), vmem_limit_bytes=VMEM_LIMIT),
        name="in_proj",
    )(x2d, norm_w, w_main, w_gz, w2_pad, gate_b, cosf, sinf)


MOBA_ONES_ROWS = 16


def _moba_select(q_ref, kmean_scr, qs_scr, which, iq, *, nb, blk, scale):
    hd = HEAD_DIM
    qf = q_ref[...].astype(F32)
    gate = _dot_nt(kmean_scr[...], qf, precision=HIGHEST)
    n_iota = lax.broadcasted_iota(I32, (nb, blk), 0)
    rank = jnp.zeros((nb, blk), I32)
    for m in range(nb):
        gm = gate[m:m + 1, :]
        beats = (gm > gate) | ((gm == gate) & (m < n_iota))
        rank = rank + jnp.where(beats & (m < iq), 1, 0)
    keep = ((n_iota < iq) & (rank < MOBA_TOPK)) | (n_iota == iq)
    maskv = jnp.where(keep, 0.0, NEG)
    maskv = jnp.concatenate([maskv, jnp.zeros((hd - nb, blk), F32)], axis=0)
    qs_scr[which, :, :hd] = (qf * scale).astype(BF16)
    qs_scr[which, :, hd:] = maskv.T.astype(BF16)


def _moba_kernel(qa_ref, qb_ref, k_ref, v_ref, oa_ref, ob_ref,
                 kmean_scr, kaug_scr, vt_scr, qs_scr, s_scr, acc_scr, *, nb, blk, scale):
    p = pl.program_id(2)
    hd = HEAD_DIM
    sub = 8

    @pl.when(p == 0)
    def _():
        col = lax.broadcasted_iota(I32, (blk, hd), 1)
        ones = jnp.ones((MOBA_ONES_ROWS, blk), BF16)
        for n in range(nb):
            rows = slice(n * blk, (n + 1) * blk)
            kb = k_ref[rows, :]
            kmean_scr[n:n + 1, :] = jnp.sum(kb.astype(F32), axis=0, keepdims=True) * (1.0 / blk)
            kaug_scr[rows, :hd] = kb
            kaug_scr[rows, hd:] = jnp.where(col == n, 1.0, 0.0).astype(BF16)
            vt_scr[n, :hd, :] = v_ref[rows, :].astype(F32).T.astype(BF16)
            vt_scr[n, hd:, :] = ones

    ia = p
    ib = nb - 1 - p
    _moba_select(qa_ref, kmean_scr, qs_scr, 0, ia, nb=nb, blk=blk, scale=scale)
    _moba_select(qb_ref, kmean_scr, qs_scr, 1, ib, nb=nb, blk=blk, scale=scale)

    def fold_max(x):
        return jnp.max(x.reshape(blk // sub, sub, blk), axis=0)

    def key_rows(jb):
        return kaug_scr[pl.ds(pl.multiple_of(jb * blk, blk), blk), :]

    kpos = lax.broadcasted_iota(I32, (blk, blk), 0)
    qpos = lax.broadcasted_iota(I32, (blk, blk), 1)
    causal = jnp.where(kpos <= qpos, 0.0, NEG)

    s_own_a = _dot_nt(key_rows(ia), qs_scr[0]) + causal
    s_own_b = _dot_nt(key_rows(ib), qs_scr[1]) + causal
    mx_a = fold_max(s_own_a)
    mx_b = fold_max(s_own_b)
    trips = []
    for t in range(nb - 1):
        is_a = t < p
        which = jnp.where(is_a, 0, 1)
        jb = jnp.where(is_a, t, t - p)
        trips.append((is_a, which, jb))
        s = _dot_nt(key_rows(jb), qs_scr[which])
        s_scr[t] = s
        f = fold_max(s)
        mx_a = jnp.maximum(mx_a, jnp.where(is_a, f, NEG))
        mx_b = jnp.maximum(mx_b, jnp.where(is_a, NEG, f))
    m_a = jnp.max(mx_a, axis=0, keepdims=True)
    m_b = jnp.max(mx_b, axis=0, keepdims=True)

    acc_scr[0] = _dot(vt_scr[ia], jnp.exp(s_own_a - m_a).astype(BF16))
    acc_scr[1] = _dot(vt_scr[ib], jnp.exp(s_own_b - m_b).astype(BF16))
    for t, (is_a, which, jb) in enumerate(trips):
        pr = jnp.exp(s_scr[t] - jnp.where(is_a, m_a, m_b))
        acc_scr[which] += _dot(vt_scr[jb], pr.astype(BF16))
    for which, o_ref in ((0, oa_ref), (1, ob_ref)):
        acc = acc_scr[which]
        o_ref[...] = (acc[:hd] / acc[hd:hd + 1]).T.astype(BF16)


def _moba(proj, batch, seq):
    blk = MOBA_BLOCK
    nb = seq // blk
    assert nb % 2 == 0 and nb <= HEAD_DIM
    half = nb // 2
    H = MOBA_HEADS
    hd = HEAD_DIM
    width = H * hd
    lo, hi = pl.pallas_call(
        functools.partial(_moba_kernel, nb=nb, blk=blk, scale=hd ** -0.5),
        grid=(batch, H, half),
        in_specs=[
            pl.BlockSpec((blk, hd), lambda b, h, p: (b * nb + p, h)),
            pl.BlockSpec((blk, hd), lambda b, h, p: (b * nb + nb - 1 - p, h)),
            pl.BlockSpec((seq, hd), lambda b, h, p: (b, H + h)),
            pl.BlockSpec((seq, hd), lambda b, h, p: (b, 2 * H + h)),
        ],
        out_specs=[
            pl.BlockSpec((blk, hd), lambda b, h, p: (b * half + p, h)),
            pl.BlockSpec((blk, hd), lambda b, h, p: (b * half + half - 1 - p, h)),
        ],
        out_shape=[jax.ShapeDtypeStruct((batch * half * blk, width), BF16)] * 2,
        scratch_shapes=[
            pltpu.VMEM((nb, hd), F32),
            pltpu.VMEM((seq, 2 * hd), BF16),
            pltpu.VMEM((nb, hd + MOBA_ONES_ROWS, blk), BF16),
            pltpu.VMEM((2, blk, 2 * hd), BF16),
            pltpu.VMEM((nb - 1, blk, blk), F32),
            pltpu.VMEM((2, hd + MOBA_ONES_ROWS, blk), F32),
        ],
        compiler_params=pltpu.CompilerParams(
            dimension_semantics=("parallel", "parallel", "arbitrary"), vmem_limit_bytes=VMEM_LIMIT),
        name="moba",
    )(proj, proj, proj, proj)
    shape3 = (batch, half * blk, width)
    return jnp.concatenate([lo.reshape(shape3), hi.reshape(shape3)], axis=1).reshape(batch * seq, width)


def _gla_levels(c):
    out = []
    m = c // 2
    while m >= 1:
        out.append(m)
        m //= 2
    return out


def _gla_kernel(q_ref, k_ref, v_ref, g_ref, la_ref, nw_ref, o_ref, st_scr, *, chunk, group, scale):
    C = chunk
    levels = _gla_levels(C)

    @pl.when(pl.program_id(2) == 0)
    def _():
        st_scr[...] = jnp.zeros_like(st_scr)

    t_io = lax.broadcasted_iota(I32, (C, C), 0)
    s_io = lax.broadcasted_iota(I32, (C, C), 1)
    sel_rows = [jnp.where(s_io <= t_io, 1.0, 0.0)]
    masks = []
    for m in levels:
        blk_t = t_io // (2 * m)
        bound = blk_t * (2 * m) + (m - 1)
        sel_rows.append(jnp.where(s_io <= bound, 1.0, 0.0))
        masks.append((blk_t == s_io // (2 * m)) & (t_io % (2 * m) >= m) & (s_io % (2 * m) < m))
    cum_mat = jnp.concatenate(sel_rows, axis=0)
    diag = t_io == s_io
    nw = nw_ref[...]

    for c in range(group):
        rows = slice(c * C, (c + 1) * C)
        la = la_ref[rows, :]
        cums = _dot(cum_mat, la, precision=HIGHEST)
        b = cums[:C]
        q = q_ref[rows, :].astype(F32) * scale
        k = k_ref[rows, :].astype(F32)
        v = v_ref[rows, :]
        a = jnp.where(diag, _dot_nt(q.astype(BF16), k.astype(BF16)), 0.0)
        for li in range(len(levels)):
            e = jnp.exp(-jnp.abs(b - cums[(li + 1) * C:(li + 2) * C]))
            a = a + jnp.where(masks[li], _dot_nt((q * e).astype(BF16), (k * e).astype(BF16)), 0.0)
        st = st_scr[...]
        o = _dot(a.astype(BF16), v) + _dot_nt((q * jnp.exp(b)).astype(BF16), st.astype(BF16))
        b_last = b[C - 1:C, :]
        ke = (k * jnp.exp(b_last - b)).astype(BF16)
        st_scr[...] = st * jnp.exp(b_last) + _dot_tn(v, ke)
        gg = g_ref[rows, :].astype(F32)
        o_ref[rows, :] = (_rms(o, nw) * (gg * jax.nn.sigmoid(gg))).astype(BF16)


def _gla(proj, la, norm_w, batch, seq):
    T = proj.shape[0]
    Hg = GLA_HEADS
    rows = GLA_CHUNK * GLA_GROUP
    ng = seq // rows
    q0 = (3 * MOBA_HEADS * HEAD_DIM) // GLA_DK
    k0 = q0 + Hg
    v0 = (3 * MOBA_HEADS * HEAD_DIM + 2 * Hg * GLA_DK) // GLA_DV
    g0 = v0 + Hg
    return pl.pallas_call(
        functools.partial(_gla_kernel, chunk=GLA_CHUNK, group=GLA_GROUP, scale=GLA_DK ** -0.5),
        grid=(batch, Hg, ng),
        in_specs=[
            pl.BlockSpec((rows, GLA_DK), lambda b, h, g: (b * ng + g, q0 + h)),
            pl.BlockSpec((rows, GLA_DK), lambda b, h, g: (b * ng + g, k0 + h)),
            pl.BlockSpec((rows, GLA_DV), lambda b, h, g: (b * ng + g, v0 + h)),
            pl.BlockSpec((rows, GLA_DV), lambda b, h, g: (b * ng + g, g0 + h)),
            pl.BlockSpec((rows, GLA_DK), lambda b, h, g: (b * ng + g, h)),
            pl.BlockSpec((1, GLA_DV), lambda b, h, g: (0, 0)),
        ],
        out_specs=pl.BlockSpec((rows, GLA_DV), lambda b, h, g: (b * ng + g, h)),
        out_shape=jax.ShapeDtypeStruct((T, Hg * GLA_DV), BF16),
        scratch_shapes=[pltpu.VMEM((GLA_DV, GLA_DK), F32)],
        compiler_params=pltpu.CompilerParams(
            dimension_semantics=("parallel", "parallel", "arbitrary"), vmem_limit_bytes=VMEM_LIMIT),
        name="gla",
    )(proj, proj, proj, proj, la, norm_w)


def _out_proj_kernel(mo_ref, gl_ref, x_ref, wo_ref, nw_ref, rw_ref, rb_ref,
                     x1_ref, h2_ref, idx_ref, gate_ref):
    wm = mo_ref.shape[1]
    x1 = x_ref[...] + _dot(mo_ref[...], wo_ref[:wm, :]) + _dot(gl_ref[...], wo_ref[wm:, :])
    x1_ref[...] = x1
    h2 = _rms(x1, nw_ref[...])
    h2_ref[...] = h2
    logits = _dot(h2, rw_ref[...], precision=HIGHEST) + rb_ref[...]
    lane = lax.broadcasted_iota(I32, logits.shape, 1)
    vals, idxs = [], []
    for _ in range(TOP_K):
        mx = jnp.max(logits, axis=-1, keepdims=True)
        ix = jnp.min(jnp.where(logits == mx, lane, LANES), axis=-1, keepdims=True)
        vals.append(mx)
        idxs.append(ix)
        logits = jnp.where(lane == ix, -jnp.inf, logits)
    exps = [jnp.exp(v - vals[0]) for v in vals]
    denom = exps[0]
    for e in exps[1:]:
        denom = denom + e
    idx_out = jnp.zeros(logits.shape, I32)
    gate_out = jnp.zeros(logits.shape, F32)
    for r in range(TOP_K):
        idx_out = jnp.where(lane == r, idxs[r], idx_out)
        gate_out = jnp.where(lane == r, exps[r] / denom, gate_out)
    idx_ref[...] = idx_out
    gate_ref[...] = gate_out


def _out_proj(moba_o, gla_o, x2d, w_out, norm_w, rw_pad, rb_pad):
    T, D = x2d.shape
    tm = PROJ_TM
    wm, wg = moba_o.shape[1], gla_o.shape[1]
    return pl.pallas_call(
        _out_proj_kernel,
        grid=(T // tm,),
        in_specs=[
            pl.BlockSpec((tm, wm), lambda i: (i, 0)),
            pl.BlockSpec((tm, wg), lambda i: (i, 0)),
            pl.BlockSpec((tm, D), lambda i: (i, 0)),
            pl.BlockSpec((wm + wg, D), lambda i: (0, 0)),
            pl.BlockSpec((1, D), lambda i: (0, 0)),
            pl.BlockSpec((D, LANES), lambda i: (0, 0)),
            pl.BlockSpec((1, LANES), lambda i: (0, 0)),
        ],
        out_specs=[
            pl.BlockSpec((tm, D), lambda i: (i, 0)),
            pl.BlockSpec((tm, D), lambda i: (i, 0)),
            pl.BlockSpec((tm, LANES), lambda i: (i, 0)),
            pl.BlockSpec((tm, LANES), lambda i: (i, 0)),
        ],
        out_shape=[
            jax.ShapeDtypeStruct((T, D), F32),
            jax.ShapeDtypeStruct((T, D), F32),
            jax.ShapeDtypeStruct((T, LANES), I32),
            jax.ShapeDtypeStruct((T, LANES), F32),
        ],
        compiler_params=pltpu.CompilerParams(
            dimension_semantics=("parallel",), vmem_limit_bytes=VMEM_LIMIT),
        name="out_proj",
    )(moba_o, gla_o, x2d, w_out, norm_w, rw_pad, rb_pad)


def _moe_row_copy(h_hbm, xbuf, sem, tok, r):
    return pltpu.make_async_copy(h_hbm.at[pl.ds(tok, 1), :], xbuf.at[pl.ds(r, 1), :], sem)


def _moe_out_copy(acc, y_hbm, sem, r, blk):
    src = acc.at[pl.ds(pl.multiple_of(r * MOE_SUB, MOE_SUB), MOE_SUB), :]
    dst = y_hbm.at[pl.ds(pl.multiple_of(blk * MOE_SUB, MOE_SUB), MOE_SUB), :]
    return pltpu.make_async_copy(src, dst, sem)


def _moe_kernel(se_ref, rs_ref, ns_ref, tok_ref,
                h_hbm, wg_ref, wu_ref, wd_ref, bg_ref, bu_ref, bd_ref, y_hbm,
                xbuf, acc, wgb, wub, wdb, gsem, osem):
    s = pl.program_id(0)
    j = pl.program_id(1)
    nj = pl.num_programs(1)
    nsub = ns_ref[s]
    blk0 = rs_ref[s]

    @pl.when(nsub > 0)
    def _():
        @pl.when(j == 0)
        def _():
            n_groups = nsub * (MOE_SUB // MOE_DMA_UNROLL)
            base = blk0 * MOE_SUB

            def issue(gidx, c):
                r0 = gidx * MOE_DMA_UNROLL
                for u in range(MOE_DMA_UNROLL):
                    _moe_row_copy(h_hbm, xbuf, gsem, tok_ref[base + r0 + u], r0 + u).start()
                return c

            lax.fori_loop(0, n_groups, issue, 0)

            bias_rows = jnp.broadcast_to(bd_ref[0], (MOE_SUB, acc.shape[1]))

            def init(r, c):
                acc[pl.ds(pl.multiple_of(r * MOE_SUB, MOE_SUB), MOE_SUB), :] = bias_rows
                return c

            lax.fori_loop(0, nsub, init, 0)

            def wait(gidx, c):
                r0 = gidx * MOE_DMA_UNROLL
                for u in range(MOE_DMA_UNROLL):
                    _moe_row_copy(h_hbm, xbuf, gsem, 0, r0 + u).wait()
                return c

            lax.fori_loop(0, n_groups, wait, 0)

        wgb[...] = wg_ref[0].astype(BF16)
        wub[...] = wu_ref[0].astype(BF16)
        wdb[...] = wd_ref[0].astype(BF16)
        bg = bg_ref[0]
        bu = bu_ref[0]

        def sub_tile(r):
            off = pl.multiple_of(r * MOE_SUB, MOE_SUB)
            xs = xbuf[pl.ds(off, MOE_SUB), :].astype(BF16)
            g = jnp.minimum(_dot(xs, wgb[...]) + bg, SWIGLU_LIMIT)
            u = jnp.clip(_dot(xs, wub[...]) + bu, -SWIGLU_LIMIT, SWIGLU_LIMIT)
            act = (u + 1.0) * (g * jax.nn.sigmoid(SWIGLU_ALPHA * g))
            acc[pl.ds(off, MOE_SUB), :] += _dot(act.astype(BF16), wdb[...])

        def pair(t, c):
            sub_tile(2 * t)
            sub_tile(2 * t + 1)
            return c

        lax.fori_loop(0, nsub >> 1, pair, 0)

        @pl.when((nsub & 1) == 1)
        def _():
            sub_tile(nsub - 1)

        @pl.when(j == nj - 1)
        def _():
            def put(r, c):
                _moe_out_copy(acc, y_hbm, osem, r, blk0 + r).start()
                return c

            lax.fori_loop(0, nsub, put, 0)

            def done(r, c):
                _moe_out_copy(acc, y_hbm, osem, r, blk0 + r).wait()
                return c

            lax.fori_loop(0, nsub, done, 0)

    @pl.when((s == pl.num_programs(0) - 1) & (j == nj - 1))
    def _():
        tail0 = rs_ref[pl.num_programs(0)]
        n_blk = y_hbm.shape[0] // MOE_SUB
        acc[0:MOE_SUB, :] = jnp.zeros((MOE_SUB, acc.shape[1]), F32)

        def put(b, c):
            _moe_out_copy(acc, y_hbm, osem, 0, b).start()
            return c

        lax.fori_loop(tail0, n_blk, put, 0)

        def done(b, c):
            _moe_out_copy(acc, y_hbm, osem, 0, b).wait()
            return c

        lax.fori_loop(tail0, n_blk, done, 0)


def _moe(sup_e, sup_blk, sup_nsub, row_tok, h2, w_gu, b_gu, w_dn, b_dn, n_rows):
    T, D = h2.shape
    E, _, F2 = w_gu.shape
    F = F2 // 2
    tf = MOE_TF
    nf = F // tf
    n_sup = sup_e.shape[0]

    def wmap(col0):
        def f(s, j, se, rs, ns, tok):
            return (se[s], 0, col0 + jnp.where(ns[s] > 0, j, nf - 1))
        return f

    def dmap(s, j, se, rs, ns, tok):
        return (se[s], jnp.where(ns[s] > 0, j, nf - 1), 0)

    def bdmap(s, j, se, rs, ns, tok):
        return (se[s], 0, 0)

    grid_spec = pltpu.PrefetchScalarGridSpec(
        num_scalar_prefetch=4,
        grid=(n_sup, nf),
        in_specs=[
            pl.BlockSpec(memory_space=pl.ANY),
            pl.BlockSpec((1, D, tf), wmap(0)),
            pl.BlockSpec((1, D, tf), wmap(nf)),
            pl.BlockSpec((1, tf, D), dmap),
            pl.BlockSpec((1, 1, tf), wmap(0)),
            pl.BlockSpec((1, 1, tf), wmap(nf)),
            pl.BlockSpec((1, 1, D), bdmap),
        ],
        out_specs=pl.BlockSpec(memory_space=pl.ANY),
        scratch_shapes=[
            pltpu.VMEM((MOE_SUP, D), F32),
            pltpu.VMEM((MOE_SUP, D), F32),
            pltpu.VMEM((D, tf), BF16),
            pltpu.VMEM((D, tf), BF16),
            pltpu.VMEM((tf, D), BF16),
            pltpu.SemaphoreType.DMA,
            pltpu.SemaphoreType.DMA,
        ],
    )
    return pl.pallas_call(
        _moe_kernel,
        grid_spec=grid_spec,
        out_shape=jax.ShapeDtypeStruct((n_rows, D), F32),
        compiler_params=pltpu.CompilerParams(
            dimension_semantics=("arbitrary", "arbitrary"), vmem_limit_bytes=VMEM_LIMIT),
        name="moe",
    )(sup_e, sup_blk, sup_nsub, row_tok, h2, w_gu, w_gu, w_dn,
      b_gu.reshape(E, 1, F2), b_gu.reshape(E, 1, F2), b_dn.reshape(E, 1, D))


def _comb_copy(y_hbm, buf, sem, row, kk, r):
    return pltpu.make_async_copy(y_hbm.at[pl.ds(row, 1), :], buf.at[kk, pl.ds(r, 1), :], sem)


def _combine_kernel(dest_ref, y_hbm, x1_ref, gate_ref, nw_ref, o_ref, buf, sem):
    i = pl.program_id(0)
    tm = x1_ref.shape[0]
    base = i * (tm * TOP_K)

    def issue(t, c):
        for u in range(COMB_UNROLL):
            r = t * COMB_UNROLL + u
            for kk in range(TOP_K):
                _comb_copy(y_hbm, buf, sem, dest_ref[base + r * TOP_K + kk], kk, r).start()
        return c

    lax.fori_loop(0, tm // COMB_UNROLL, issue, 0)

    def wait(t, c):
        for u in range(COMB_UNROLL):
            r = t * COMB_UNROLL + u
            for kk in range(TOP_K):
                _comb_copy(y_hbm, buf, sem, 0, kk, r).wait()
        return c

    lax.fori_loop(0, tm // COMB_UNROLL, wait, 0)

    gates = gate_ref[...]
    x2 = x1_ref[...]
    for kk in range(TOP_K):
        x2 = x2 + gates[:, kk:kk + 1] * buf[kk]
    o_ref[...] = _rms(x2, nw_ref[...])


def _combine(dest, y_rows, x1, gates, norm_w):
    T, D = x1.shape
    tm = COMB_TM
    grid_spec = pltpu.PrefetchScalarGridSpec(
        num_scalar_prefetch=1,
        grid=(T // tm,),
        in_specs=[
            pl.BlockSpec(memory_space=pl.ANY),
            pl.BlockSpec((tm, D), lambda i, d: (i, 0)),
            pl.BlockSpec((tm, LANES), lambda i, d: (i, 0)),
            pl.BlockSpec((1, D), lambda i, d: (0, 0)),
        ],
        out_specs=pl.BlockSpec((tm, D), lambda i, d: (i, 0)),
        scratch_shapes=[pltpu.VMEM((TOP_K, tm, D), F32), pltpu.SemaphoreType.DMA],
    )
    return pl.pallas_call(
        _combine_kernel,
        grid_spec=grid_spec,
        out_shape=jax.ShapeDtypeStruct((T, D), F32),
        compiler_params=pltpu.CompilerParams(
            dimension_semantics=("arbitrary",), vmem_limit_bytes=VMEM_LIMIT),
        name="combine",
    )(dest, y_rows, x1, gates, norm_w)


def _route(top_idx, n_experts):
    T = top_idx.shape[0]
    A = T * TOP_K
    G, SUP = MOE_SUB, MOE_SUP
    flat_e = top_idx.reshape(A)
    onehot = (flat_e[:, None] == jnp.arange(n_experts, dtype=I32)[None, :]).astype(I32)
    csum = jnp.cumsum(onehot, axis=0)
    counts = csum[-1]
    rank = jnp.sum((csum - 1) * onehot, axis=1)
    pcounts = (counts + G - 1) // G * G
    pends = jnp.cumsum(pcounts)
    pstarts = pends - pcounts
    dest = (pstarts[flat_e] + rank).astype(I32)
    n_rows = (A + n_experts * (G - 1)) // G * G
    row_tok = jnp.zeros((n_rows,), I32).at[dest].set(jnp.arange(A, dtype=I32) // TOP_K)

    n_sup = (A + n_experts * (G - 1) + n_experts * (SUP - G)) // SUP
    nsup_e = (pcounts + SUP - 1) // SUP
    sup_ends = jnp.cumsum(nsup_e)
    total = sup_ends[-1]
    s_ids = jnp.arange(n_sup, dtype=I32)
    s_eff = jnp.minimum(s_ids, total - 1)
    e_of = jnp.sum((s_eff[:, None] >= sup_ends[None, :]).astype(I32), axis=1)
    local = s_eff - (sup_ends - nsup_e)[e_of]
    active = s_ids < total
    blk0 = (pstarts[e_of] + local * SUP) // G
    nsub = jnp.minimum(SUP // G, (pcounts[e_of] - local * SUP) // G)
    sup_e = e_of.astype(I32)
    sup_blk = jnp.concatenate([jnp.where(active, blk0, 0), pends[-1:] // G]).astype(I32)
    sup_nsub = jnp.where(active, nsub, 0).astype(I32)
    return dest, row_tok, n_rows, sup_e, sup_blk, sup_nsub


def _rope_tables(seq):
    half = HEAD_DIM // 2
    inv_freq = 1.0 / (ROPE_THETA ** (jnp.arange(0, HEAD_DIM, 2, dtype=F32) / HEAD_DIM))
    ang = jnp.arange(seq, dtype=F32)[:, None] * inv_freq[None, :]
    cos, sin = jnp.cos(ang), jnp.sin(ang)
    assert cos.shape[1] == half
    return jnp.concatenate([cos, cos], axis=1), jnp.concatenate([-sin, sin], axis=1)


def kernel(x, attn_norm_w, w_in, gla_gate_w2, gla_gate_b, gla_norm_w, w_out, ffn_norm_w,
           router_w, router_b, w_gate_up, b_gate_up, w_down, b_down, final_norm_w):
    B, S, D = x.shape
    T = B * S
    assert w_in.shape[0] == 1, "single-layer stack: the final norm is fused into combine"
    n_experts = router_w.shape[-1]
    n_main = w_in.shape[-1] - GLA_GATE_RANK
    cosf, sinf = _rope_tables(S)
    x2d = x.reshape(T, D)
    w_main = w_in[0, :, :n_main].astype(BF16)
    w_gz = jnp.pad(w_in[0, :, n_main:], ((0, 0), (0, LANES - GLA_GATE_RANK))).astype(BF16)
    w2_pad = jnp.pad(gla_gate_w2[0], ((0, LANES - GLA_GATE_RANK), (0, 0)))
    proj, log_a = _in_proj(x2d, attn_norm_w[0][None, :], w_main, w_gz, w2_pad,
                           gla_gate_b[0][None, :], cosf, sinf, S)
    moba_o = _moba(proj, B, S)
    gla_o = _gla(proj, log_a, gla_norm_w[0][None, :], B, S)
    rw_pad = jnp.pad(router_w[0], ((0, 0), (0, LANES - n_experts)))
    rb_pad = jnp.pad(router_b[0], (0, LANES - n_experts), constant_values=NEG)[None, :]
    x1, h2, top_idx, gates = _out_proj(moba_o, gla_o, x2d, w_out[0].astype(BF16),
                                       ffn_norm_w[0][None, :], rw_pad, rb_pad)
    dest, row_tok, n_rows, sup_e, sup_blk, sup_nsub = _route(top_idx[:, :TOP_K], n_experts)
    y_rows = _moe(sup_e, sup_blk, sup_nsub, row_tok, h2, w_gate_up[0], b_gate_up[0],
                  w_down[0], b_down[0], n_rows)
    out = _combine(dest, y_rows, x1, gates, final_norm_w[None, :])
    return out.reshape(B, S, D)
```

```python
import functools

import jax
import jax.numpy as jnp
from jax import lax
from jax.experimental import pallas as pl
from jax.experimental.pallas import tpu as pltpu

F32 = jnp.float32
BF16 = jnp.bfloat16
I32 = jnp.int32
U32 = jnp.uint32
HIGHEST = lax.Precision.HIGHEST

HEAD_DIM = 128
MOBA_HEADS = 8
MOBA_BLOCK = 256
MOBA_TOPK = 3
ROPE_THETA = 10000.0
GLA_HEADS = 4
GLA_DK = 128
GLA_DV = 256
GLA_GATE_RANK = 16
GLA_GATE_NORMALIZER = 16.0
TOP_K = 4
SWIGLU_ALPHA = 1.702
SWIGLU_LIMIT = 7.0
NORM_EPS = 1e-5

LANES = 128
VMEM_LIMIT = 58 * 1024 * 1024
NEG = -1e30

PROJ_TM = 512
PROJ_TN = 1024
GLA_CHUNK = 64
GLA_GROUP = 8
MOE_SUB = 256
MOE_SUP = 2048
MOE_TF = 256
MOE_DMA_UNROLL = 8
COMB_TM = 256
COMB_UNROLL = 2


def _dot(a, b, **kw):
    return jnp.dot(a, b, preferred_element_type=F32, **kw)


def _dot_nt(a, b, **kw):
    return lax.dot_general(a, b, (((1,), (1,)), ((), ())), preferred_element_type=F32, **kw)


def _dot_tn(a, b, **kw):
    return lax.dot_general(a, b, (((0,), (0,)), ((), ())), preferred_element_type=F32, **kw)


def _rms(x, w):
    return x * lax.rsqrt(jnp.mean(x * x, axis=-1, keepdims=True) + NORM_EPS) * w


def _in_proj_kernel(x_ref, nw_ref, w_ref, wz_ref, w2_ref, gb_ref, cos_ref, sin_ref,
                    proj_ref, la_ref, h_scr, *, n_main, n_rope):
    j = pl.program_id(1)

    @pl.when(j == 0)
    def _():
        h_scr[...] = _rms(x_ref[...], nw_ref[...]).astype(BF16)

    @pl.when(j < n_rope)
    def _():
        acc = _dot(h_scr[...], w_ref[...])
        cosf = cos_ref[...]
        sinf = sin_ref[...]
        for hd in range(acc.shape[1] // HEAD_DIM):
            t = acc[:, hd * HEAD_DIM:(hd + 1) * HEAD_DIM]
            r = t * cosf + pltpu.roll(t, HEAD_DIM // 2, axis=1) * sinf
            proj_ref[:, hd * HEAD_DIM:(hd + 1) * HEAD_DIM] = r.astype(BF16)

    @pl.when((j >= n_rope) & (j < n_main))
    def _():
        proj_ref[...] = _dot(h_scr[...], w_ref[...]).astype(BF16)

    @pl.when(j == n_main)
    def _():
        gz = _dot(h_scr[...], wz_ref[...])
        logits = _dot(gz, w2_ref[...], precision=HIGHEST) + gb_ref[...]
        logsig = jnp.minimum(logits, 0.0) - jnp.log1p(jnp.exp(-jnp.abs(logits)))
        la_ref[...] = logsig * (1.0 / GLA_GATE_NORMALIZER)


def _in_proj(x2d, norm_w, w_main, w_gz, w2_pad, gate_b, cosf, sinf, seq):
    T, D = x2d.shape
    n_cols = w_main.shape[1]
    tm, tn = PROJ_TM, PROJ_TN
    n_main = n_cols // tn
    n_rope = (2 * MOBA_HEADS * HEAD_DIM) // tn
    kw = w2_pad.shape[1]
    s_tiles = seq // tm
    last = n_main - 1
    return pl.pallas_call(
        functools.partial(_in_proj_kernel, n_main=n_main, n_rope=n_rope),
        grid=(T // tm, n_main + 1),
        in_specs=[
            pl.BlockSpec((tm, D), lambda i, j: (i, 0)),
            pl.BlockSpec((1, D), lambda i, j: (0, 0)),
            pl.BlockSpec((D, tn), lambda i, j: (0, jnp.minimum(j, last))),
            pl.BlockSpec((D, LANES), lambda i, j: (0, 0)),
            pl.BlockSpec((LANES, kw), lambda i, j: (0, 0)),
            pl.BlockSpec((1, kw), lambda i, j: (0, 0)),
            pl.BlockSpec((tm, HEAD_DIM), lambda i, j: (i % s_tiles, 0)),
            pl.BlockSpec((tm, HEAD_DIM), lambda i, j: (i % s_tiles, 0)),
        ],
        out_specs=[
            pl.BlockSpec((tm, tn), lambda i, j: (i, jnp.minimum(j, last))),
            pl.BlockSpec((tm, kw), lambda i, j: (i, 0)),
        ],
        out_shape=[
            jax.ShapeDtypeStruct((T, n_cols), BF16),
            jax.ShapeDtypeStruct((T, kw), F32),
        ],
        scratch_shapes=[pltpu.VMEM((tm, D), BF16)],
        compiler_params=pltpu.CompilerParams(
            dimension_semantics=("parallel", "arbitrary"), vmem_limit_bytes=VMEM_LIMIT),
        name="in_proj",
    )(x2d, norm_w, w_main, w_gz, w2_pad, gate_b, cosf, sinf)


MOBA_ONES_ROWS = 16


def _moba_select(q_ref, kmean_scr, qs_scr, which, iq, *, nb, blk, scale):
    hd = HEAD_DIM
    qf = q_ref[...].astype(F32)
    gate = _dot_nt(kmean_scr[...], qf, precision=HIGHEST)
    n_iota = lax.broadcasted_iota(I32, (nb, blk), 0)
    rank = jnp.zeros((nb, blk), I32)
    for m in range(nb):
        gm = gate[m:m + 1, :]
        beats = (gm > gate) | ((gm == gate) & (m < n_iota))
        rank = rank + jnp.where(beats & (m < iq), 1, 0)
    keep = ((n_iota < iq) & (rank < MOBA_TOPK)) | (n_iota == iq)
    maskv = jnp.where(keep, 0.0, NEG)
    maskv = jnp.concatenate([maskv, jnp.zeros((hd - nb, blk), F32)], axis=0)
    qs_scr[which, :, :hd] = (qf * scale).astype(BF16)
    qs_scr[which, :, hd:] = maskv.T.astype(BF16)


def _moba_kernel(qa_ref, qb_ref, k_ref, v_ref, oa_ref, ob_ref,
                 kmean_scr, kaug_scr, vt_scr, qs_scr, s_scr, acc_scr, *, nb, blk, scale):
    p = pl.program_id(2)
    hd = HEAD_DIM
    sub = 8

    @pl.when(p == 0)
    def _():
        col = lax.broadcasted_iota(I32, (blk, hd), 1)
        ones = jnp.ones((MOBA_ONES_ROWS, blk), BF16)
        for n in range(nb):
            rows = slice(n * blk, (n + 1) * blk)
            kb = k_ref[rows, :]
            kmean_scr[n:n + 1, :] = jnp.sum(kb.astype(F32), axis=0, keepdims=True) * (1.0 / blk)
            kaug_scr[rows, :hd] = kb
            kaug_scr[rows, hd:] = jnp.where(col == n, 1.0, 0.0).astype(BF16)
            vt_scr[n, :hd, :] = v_ref[rows, :].astype(F32).T.astype(BF16)
            vt_scr[n, hd:, :] = ones

    ia = p
    ib = nb - 1 - p
    _moba_select(qa_ref, kmean_scr, qs_scr, 0, ia, nb=nb, blk=blk, scale=scale)
    _moba_select(qb_ref, kmean_scr, qs_scr, 1, ib, nb=nb, blk=blk, scale=scale)

    def fold_max(x):
        return jnp.max(x.reshape(blk // sub, sub, blk), axis=0)

    def key_rows(jb):
        return kaug_scr[pl.ds(pl.multiple_of(jb * blk, blk), blk), :]

    kpos = lax.broadcasted_iota(I32, (blk, blk), 0)
    qpos = lax.broadcasted_iota(I32, (blk, blk), 1)
    causal = jnp.where(kpos <= qpos, 0.0, NEG)

    s_own_a = _dot_nt(key_rows(ia), qs_scr[0]) + causal
    s_own_b = _dot_nt(key_rows(ib), qs_scr[1]) + causal
    mx_a = fold_max(s_own_a)
    mx_b = fold_max(s_own_b)
    trips = []
    for t in range(nb - 1):
        is_a = t < p
        which = jnp.where(is_a, 0, 1)
        jb = jnp.where(is_a, t, t - p)
        trips.append((is_a, which, jb))
        s = _dot_nt(key_rows(jb), qs_scr[which])
        s_scr[t] = s
        f = fold_max(s)
        mx_a = jnp.maximum(mx_a, jnp.where(is_a, f, NEG))
        mx_b = jnp.maximum(mx_b, jnp.where(is_a, NEG, f))
    m_a = jnp.max(mx_a, axis=0, keepdims=True)
    m_b = jnp.max(mx_b, axis=0, keepdims=True)

    acc_scr[0] = _dot(vt_scr[ia], jnp.exp(s_own_a - m_a).astype(BF16))
    acc_scr[1] = _dot(vt_scr[ib], jnp.exp(s_own_b - m_b).astype(BF16))
    for t, (is_a, which, jb) in enumerate(trips):
        pr = jnp.exp(s_scr[t] - jnp.where(is_a, m_a, m_b))
        acc_scr[which] += _dot(vt_scr[jb], pr.astype(BF16))
    for which, o_ref in ((0, oa_ref), (1, ob_ref)):
        acc = acc_scr[which]
        o_ref[...] = (acc[:hd] / acc[hd:hd + 1]).T.astype(BF16)


def _moba(proj, batch, seq):
    blk = MOBA_BLOCK
    nb = seq // blk
    assert nb % 2 == 0 and nb <= HEAD_DIM
    half = nb // 2
    H = MOBA_HEADS
    hd = HEAD_DIM
    width = H * hd
    lo, hi = pl.pallas_call(
        functools.partial(_moba_kernel, nb=nb, blk=blk, scale=hd ** -0.5),
        grid=(batch, H, half),
        in_specs=[
            pl.BlockSpec((blk, hd), lambda b, h, p: (b * nb + p, h)),
            pl.BlockSpec((blk, hd), lambda b, h, p: (b * nb + nb - 1 - p, h)),
            pl.BlockSpec((seq, hd), lambda b, h, p: (b, H + h)),
            pl.BlockSpec((seq, hd), lambda b, h, p: (b, 2 * H + h)),
        ],
        out_specs=[
            pl.BlockSpec((blk, hd), lambda b, h, p: (b * half + p, h)),
            pl.BlockSpec((blk, hd), lambda b, h, p: (b * half + half - 1 - p, h)),
        ],
        out_shape=[jax.ShapeDtypeStruct((batch * half * blk, width), BF16)] * 2,
        scratch_shapes=[
            pltpu.VMEM((nb, hd), F32),
            pltpu.VMEM((seq, 2 * hd), BF16),
            pltpu.VMEM((nb, hd + MOBA_ONES_ROWS, blk), BF16),
            pltpu.VMEM((2, blk, 2 * hd), BF16),
            pltpu.VMEM((nb - 1, blk, blk), F32),
            pltpu.VMEM((2, hd + MOBA_ONES_ROWS, blk), F32),
        ],
        compiler_params=pltpu.CompilerParams(
            dimension_semantics=("parallel", "parallel", "arbitrary"), vmem_limit_bytes=VMEM_LIMIT),
        name="moba",
    )(proj, proj, proj, proj)
    shape3 = (batch, half * blk, width)
    return jnp.concatenate([lo.reshape(shape3), hi.reshape(shape3)], axis=1).reshape(batch * seq, width)


def _gla_levels(c):
    out = []
    m = c // 2
    while m >= 1:
        out.append(m)
        m //= 2
    return out


def _gla_kernel(q_ref, k_ref, v_ref, g_ref, la_ref, nw_ref, o_ref, st_scr, *, chunk, group, scale):
    C = chunk
    levels = _gla_levels(C)

    @pl.when(pl.program_id(2) == 0)
    def _():
        st_scr[...] = jnp.zeros_like(st_scr)

    t_io = lax.broadcasted_iota(I32, (C, C), 0)
    s_io = lax.broadcasted_iota(I32, (C, C), 1)
    sel_rows = [jnp.where(s_io <= t_io, 1.0, 0.0)]
    masks = []
    for m in levels:
        blk_t = t_io // (2 * m)
        bound = blk_t * (2 * m) + (m - 1)
        sel_rows.append(jnp.where(s_io <= bound, 1.0, 0.0))
        masks.append((blk_t == s_io // (2 * m)) & (t_io % (2 * m) >= m) & (s_io % (2 * m) < m))
    cum_mat = jnp.concatenate(sel_rows, axis=0)
    diag = t_io == s_io
    nw = nw_ref[...]

    for c in range(group):
        rows = slice(c * C, (c + 1) * C)
        la = la_ref[rows, :]
        cums = _dot(cum_mat, la, precision=HIGHEST)
        b = cums[:C]
        q = q_ref[rows, :].astype(F32) * scale
        k = k_ref[rows, :].astype(F32)
        v = v_ref[rows, :]
        a = jnp.where(diag, _dot_nt(q.astype(BF16), k.astype(BF16)), 0.0)
        for li in range(len(levels)):
            e = jnp.exp(-jnp.abs(b - cums[(li + 1) * C:(li + 2) * C]))
            a = a + jnp.where(masks[li], _dot_nt((q * e).astype(BF16), (k * e).astype(BF16)), 0.0)
        st = st_scr[...]
        o = _dot(a.astype(BF16), v) + _dot_nt((q * jnp.exp(b)).astype(BF16), st.astype(BF16))
        b_last = b[C - 1:C, :]
        ke = (k * jnp.exp(b_last - b)).astype(BF16)
        st_scr[...] = st * jnp.exp(b_last) + _dot_tn(v, ke)
        gg = g_ref[rows, :].astype(F32)
        o_ref[rows, :] = (_rms(o, nw) * (gg * jax.nn.sigmoid(gg))).astype(BF16)


def _gla(proj, la, norm_w, batch, seq):
    T = proj.shape[0]
    Hg = GLA_HEADS
    rows = GLA_CHUNK * GLA_GROUP
    ng = seq // rows
    q0 = (3 * MOBA_HEADS * HEAD_DIM) // GLA_DK
    k0 = q0 + Hg
    v0 = (3 * MOBA_HEADS * HEAD_DIM + 2 * Hg * GLA_DK) // GLA_DV
    g0 = v0 + Hg
    return pl.pallas_call(
        functools.partial(_gla_kernel, chunk=GLA_CHUNK, group=GLA_GROUP, scale=GLA_DK ** -0.5),
        grid=(batch, Hg, ng),
        in_specs=[
            pl.BlockSpec((rows, GLA_DK), lambda b, h, g: (b * ng + g, q0 + h)),
            pl.BlockSpec((rows, GLA_DK), lambda b, h, g: (b * ng + g, k0 + h)),
            pl.BlockSpec((rows, GLA_DV), lambda b, h, g: (b * ng + g, v0 + h)),
            pl.BlockSpec((rows, GLA_DV), lambda b, h, g: (b * ng + g, g0 + h)),
            pl.BlockSpec((rows, GLA_DK), lambda b, h, g: (b * ng + g, h)),
            pl.BlockSpec((1, GLA_DV), lambda b, h, g: (0, 0)),
        ],
        out_specs=pl.BlockSpec((rows, GLA_DV), lambda b, h, g: (b * ng + g, h)),
        out_shape=jax.ShapeDtypeStruct((T, Hg * GLA_DV), BF16),
        scratch_shapes=[pltpu.VMEM((GLA_DV, GLA_DK), F32)],
        compiler_params=pltpu.CompilerParams(
            dimension_semantics=("parallel", "parallel", "arbitrary"), vmem_limit_bytes=VMEM_LIMIT),
        name="gla",
    )(proj, proj, proj, proj, la, norm_w)


def _split_bf16(x):
    hi = x.astype(BF16)
    return hi, (x - hi.astype(F32)).astype(BF16)


def _pack_bf16_pairs(x):
    n = x.shape[1] // 2
    lo = lax.bitcast_convert_type(x[:, :n].astype(BF16).astype(F32), U32)
    hi = lax.bitcast_convert_type(x[:, n:].astype(BF16).astype(F32), U32)
    return (lo >> 16) | (hi & jnp.uint32(0xFFFF0000))


def _unpack_bf16_pairs(p):
    lo = lax.bitcast_convert_type(p << 16, F32).astype(BF16)
    hi = lax.bitcast_convert_type(p & jnp.uint32(0xFFFF0000), F32).astype(BF16)
    return lo, hi


def _out_proj_kernel(mo_ref, gl_ref, x_ref, wo_ref, nw_ref, rwh_ref, rwl_ref, rb_ref,
                     x1_ref, h2p_ref, idx_ref, gate_ref):
    wm = mo_ref.shape[1]
    x1 = x_ref[...] + _dot(mo_ref[...], wo_ref[:wm, :]) + _dot(gl_ref[...], wo_ref[wm:, :])
    x1_ref[...] = x1
    h2 = _rms(x1, nw_ref[...])
    h2p_ref[...] = _pack_bf16_pairs(h2)
    h_hi, h_lo = _split_bf16(h2)
    rw_hi = rwh_ref[...]
    logits = _dot_nt(rw_hi, h_hi) + (_dot_nt(rwl_ref[...], h_hi) + _dot_nt(rw_hi, h_lo))
    tm = logits.shape[1]
    logits = logits + jnp.concatenate([rb_ref[...]] * (tm // LANES), axis=1)
    row = lax.broadcasted_iota(I32, logits.shape, 0)
    vals, idxs = [], []
    for _ in range(TOP_K):
        mx = jnp.max(logits, axis=0, keepdims=True)
        ix = jnp.min(jnp.where(logits == mx, row, LANES), axis=0, keepdims=True)
        vals.append(mx)
        idxs.append(ix)
        logits = jnp.where(row == ix, -jnp.inf, logits)
    exps = [jnp.exp(v - vals[0]) for v in vals]
    denom = exps[0]
    for e in exps[1:]:
        denom = denom + e
    idx_t = jnp.zeros(logits.shape, I32)
    gate_t = jnp.zeros(logits.shape, F32)
    for r in range(TOP_K):
        idx_t = jnp.where(row == r, idxs[r], idx_t)
        gate_t = jnp.where(row == r, exps[r] / denom, gate_t)
    idx_ref[...] = idx_t.astype(F32).T.astype(I32)
    gate_ref[...] = gate_t.T


def _out_proj(moba_o, gla_o, x2d, w_out, norm_w, rw_hi, rw_lo, rb_tile):
    T, D = x2d.shape
    tm = PROJ_TM
    wm, wg = moba_o.shape[1], gla_o.shape[1]
    return pl.pallas_call(
        _out_proj_kernel,
        grid=(T // tm,),
        in_specs=[
            pl.BlockSpec((tm, wm), lambda i: (i, 0)),
            pl.BlockSpec((tm, wg), lambda i: (i, 0)),
            pl.BlockSpec((tm, D), lambda i: (i, 0)),
            pl.BlockSpec((wm + wg, D), lambda i: (0, 0)),
            pl.BlockSpec((1, D), lambda i: (0, 0)),
            pl.BlockSpec((LANES, D), lambda i: (0, 0)),
            pl.BlockSpec((LANES, D), lambda i: (0, 0)),
            pl.BlockSpec((LANES, LANES), lambda i: (0, 0)),
        ],
        out_specs=[
            pl.BlockSpec((tm, D), lambda i: (i, 0)),
            pl.BlockSpec((tm, D // 2), lambda i: (i, 0)),
            pl.BlockSpec((tm, LANES), lambda i: (i, 0)),
            pl.BlockSpec((tm, LANES), lambda i: (i, 0)),
        ],
        out_shape=[
            jax.ShapeDtypeStruct((T, D), F32),
            jax.ShapeDtypeStruct((T, D // 2), U32),
            jax.ShapeDtypeStruct((T, LANES), I32),
            jax.ShapeDtypeStruct((T, LANES), F32),
        ],
        compiler_params=pltpu.CompilerParams(
            dimension_semantics=("parallel",), vmem_limit_bytes=VMEM_LIMIT),
        name="out_proj",
    )(moba_o, gla_o, x2d, w_out, norm_w, rw_hi, rw_lo, rb_tile)


def _moe_row_in(h_hbm, xbuf, sem, slot, tok, r):
    return pltpu.make_async_copy(h_hbm.at[pl.ds(tok, 1), :], xbuf.at[slot, pl.ds(r, 1), :],
                                 sem.at[slot])


def _moe_row_out(acc, y_hbm, sem, r, dst_row):
    return pltpu.make_async_copy(acc.at[pl.ds(r, 1), :], y_hbm.at[pl.ds(dst_row, 1), :], sem)


def _moe_blk_out(acc, y_hbm, sem, blk):
    dst = y_hbm.at[pl.ds(pl.multiple_of(blk * MOE_SUB, MOE_SUB), MOE_SUB), :]
    return pltpu.make_async_copy(acc.at[pl.ds(0, MOE_SUB), :], dst, sem)


def _moe_kernel(se_ref, rs_ref, ns_ref, asg_ref,
                h_hbm, wg_ref, wu_ref, wd_ref, bg_ref, bu_ref, bd_ref, y_hbm,
                xbuf, acc, wgb, wub, wdb, gsem, osem, *, n_tok, nf):
    s = pl.program_id(0)
    j = pl.program_id(1)
    n_grid = pl.num_programs(0)
    nsub = ns_ref[s]
    base = rs_ref[s] * MOE_SUB
    slot = s & 1
    tok_shift = TOP_K.bit_length() - 1
    per_trip = MOE_SUB // nf
    half = wgb.shape[0] // 2

    def src_token(p):
        return jnp.minimum(asg_ref[p] >> tok_shift, n_tok - 1)

    def wait_rows(n_sub_tiles):
        def wait(gidx, c):
            for u in range(MOE_DMA_UNROLL):
                _moe_row_in(h_hbm, xbuf, gsem, slot, 0, gidx * MOE_DMA_UNROLL + u).wait()
            return c

        lax.fori_loop(0, n_sub_tiles * (MOE_SUB // MOE_DMA_UNROLL), wait, 0)

    @pl.when(j == 0)
    def _():
        @pl.when(s == 0)
        def _():
            def issue(gidx, c):
                for u in range(MOE_DMA_UNROLL):
                    r = gidx * MOE_DMA_UNROLL + u
                    _moe_row_in(h_hbm, xbuf, gsem, slot, src_token(base + r), r).start()
                return c

            lax.fori_loop(0, nsub * (MOE_SUB // MOE_DMA_UNROLL), issue, 0)

        wait_rows(jnp.where(s == 0, nsub, ns_ref[jnp.maximum(s - 1, 0)]))

    @pl.when(nsub > 0)
    def _():
        @pl.when(j == 0)
        def _():
            bias_rows = jnp.broadcast_to(bd_ref[0], (MOE_SUB, acc.shape[1]))

            def init(r, c):
                acc[pl.ds(pl.multiple_of(r * MOE_SUB, MOE_SUB), MOE_SUB), :] = bias_rows
                return c

            lax.fori_loop(0, nsub, init, 0)

        wgb[...] = wg_ref[0].astype(BF16)
        wub[...] = wu_ref[0].astype(BF16)
        wdb[...] = wd_ref[0].astype(BF16)
        bg = bg_ref[0]
        bu = bu_ref[0]
        next_base = rs_ref[s + 1] * MOE_SUB

        def sub_tile(r, emit):
            off = pl.multiple_of(r * MOE_SUB, MOE_SUB)
            xl, xh = _unpack_bf16_pairs(xbuf[slot, pl.ds(off, MOE_SUB), :])
            g = _dot(xl, wgb[:half, :]) + _dot(xh, wgb[half:, :]) + bg
            u = _dot(xl, wub[:half, :]) + _dot(xh, wub[half:, :]) + bu
            g = jnp.minimum(g, SWIGLU_LIMIT)
            u = jnp.clip(u, -SWIGLU_LIMIT, SWIGLU_LIMIT)
            act = (u + 1.0) * (g * jax.nn.sigmoid(SWIGLU_ALPHA * g))
            acc[pl.ds(off, MOE_SUB), :] += _dot(act.astype(BF16), wdb[...])
            g0 = pl.multiple_of((j * nsub + r) * per_trip, per_trip)
            x_next = xbuf.at[1 - slot, pl.ds(g0, per_trip), :]
            for v in range(per_trip):
                tok = src_token(next_base + g0 + v)
                pltpu.make_async_copy(h_hbm.at[pl.ds(tok, 1), :], x_next.at[pl.ds(v, 1), :],
                                      gsem.at[1 - slot]).start()
            if emit:
                acc_rows = acc.at[pl.ds(off, MOE_SUB), :]
                for v in range(MOE_SUB):
                    _moe_row_out(acc_rows, y_hbm, osem, v, asg_ref[base + off + v]).start()

        def sweep(emit):
            def pair(t, c):
                sub_tile(2 * t, emit)
                sub_tile(2 * t + 1, emit)
                return c

            lax.fori_loop(0, nsub >> 1, pair, 0)

            @pl.when((nsub & 1) == 1)
            def _():
                sub_tile(nsub - 1, emit)

        @pl.when(j < nf - 1)
        def _():
            sweep(False)

        @pl.when(j == nf - 1)
        def _():
            sweep(True)

            def done(gidx, c):
                for u in range(MOE_DMA_UNROLL):
                    _moe_row_out(acc, y_hbm, osem, 0, 0).wait()
                return c

            lax.fori_loop(0, nsub * (MOE_SUB // MOE_DMA_UNROLL), done, 0)

    @pl.when((s == n_grid - 1) & (j == nf - 1))
    def _():
        tail0 = rs_ref[n_grid]
        n_blk = y_hbm.shape[0] // MOE_SUB
        acc[0:MOE_SUB, :] = jnp.zeros((MOE_SUB, acc.shape[1]), F32)

        def put(b, c):
            _moe_blk_out(acc, y_hbm, osem, b).start()
            return c

        lax.fori_loop(tail0, n_blk, put, 0)

        def done(b, c):
            _moe_blk_out(acc, y_hbm, osem, b).wait()
            return c

        lax.fori_loop(tail0, n_blk, done, 0)


def _moe(sup_e, sup_blk, sup_nsub, row_asg, h2p, w_gu, b_gu, w_dn, b_dn, n_rows):
    T, DP = h2p.shape
    D = 2 * DP
    E, _, F2 = w_gu.shape
    F = F2 // 2
    tf = MOE_TF
    nf = F // tf
    assert MOE_SUB % nf == 0 and TOP_K & (TOP_K - 1) == 0
    n_sup = sup_e.shape[0]

    def wmap(col0):
        def f(s, j, se, rs, ns, tok):
            return (se[s], 0, col0 + jnp.where(ns[s] > 0, j, nf - 1))
        return f

    def dmap(s, j, se, rs, ns, tok):
        return (se[s], jnp.where(ns[s] > 0, j, nf - 1), 0)

    def bdmap(s, j, se, rs, ns, tok):
        return (se[s], 0, 0)

    grid_spec = pltpu.PrefetchScalarGridSpec(
        num_scalar_prefetch=4,
        grid=(n_sup, nf),
        in_specs=[
            pl.BlockSpec(memory_space=pl.ANY),
            pl.BlockSpec((1, D, tf), wmap(0)),
            pl.BlockSpec((1, D, tf), wmap(nf)),
            pl.BlockSpec((1, tf, D), dmap),
            pl.BlockSpec((1, 1, tf), wmap(0)),
            pl.BlockSpec((1, 1, tf), wmap(nf)),
            pl.BlockSpec((1, 1, D), bdmap),
        ],
        out_specs=pl.BlockSpec(memory_space=pl.ANY),
        scratch_shapes=[
            pltpu.VMEM((2, MOE_SUP, DP), U32),
            pltpu.VMEM((MOE_SUP, D), F32),
            pltpu.VMEM((D, tf), BF16),
            pltpu.VMEM((D, tf), BF16),
            pltpu.VMEM((tf, D), BF16),
            pltpu.SemaphoreType.DMA((2,)),
            pltpu.SemaphoreType.DMA,
        ],
    )
    return pl.pallas_call(
        functools.partial(_moe_kernel, n_tok=T, nf=nf),
        grid_spec=grid_spec,
        out_shape=jax.ShapeDtypeStruct((n_rows, D), F32),
        compiler_params=pltpu.CompilerParams(
            dimension_semantics=("arbitrary", "arbitrary"), vmem_limit_bytes=VMEM_LIMIT),
        name="moe",
    )(sup_e, sup_blk, sup_nsub, row_asg, h2p, w_gu, w_gu, w_dn,
      b_gu.reshape(E, 1, F2), b_gu.reshape(E, 1, F2), b_dn.reshape(E, 1, D))


def _combine_kernel(y_ref, x1_ref, gate_ref, nw_ref, o_ref):
    gates = gate_ref[...]
    x2 = x1_ref[...]
    d = x2.shape[1]
    for kk in range(TOP_K):
        x2 = x2 + gates[:, kk:kk + 1] * y_ref[:, kk * d:(kk + 1) * d]
    o_ref[...] = _rms(x2, nw_ref[...])


def _combine(y_slots, x1, gates, norm_w):
    T, D = x1.shape
    tm = COMB_TM
    y_tok = y_slots.reshape(y_slots.shape[0] // TOP_K, TOP_K * D)
    return pl.pallas_call(
        _combine_kernel,
        grid=(T // tm,),
        in_specs=[
            pl.BlockSpec((tm, TOP_K * D), lambda i: (i, 0)),
            pl.BlockSpec((tm, D), lambda i: (i, 0)),
            pl.BlockSpec((tm, LANES), lambda i: (i, 0)),
            pl.BlockSpec((1, D), lambda i: (0, 0)),
        ],
        out_specs=pl.BlockSpec((tm, D), lambda i: (i, 0)),
        out_shape=jax.ShapeDtypeStruct((T, D), F32),
        compiler_params=pltpu.CompilerParams(
            dimension_semantics=("parallel",), vmem_limit_bytes=VMEM_LIMIT),
        name="combine",
    )(y_tok, x1, gates, norm_w)


def _route(top_idx, n_experts):
    T = top_idx.shape[0]
    A = T * TOP_K
    G, SUP = MOE_SUB, MOE_SUP
    flat_e = top_idx.reshape(A)
    onehot = (flat_e[:, None] == jnp.arange(n_experts, dtype=I32)[None, :]).astype(I32)
    csum = jnp.cumsum(onehot, axis=0)
    counts = csum[-1]
    rank = jnp.sum((csum - 1) * onehot, axis=1)
    pcounts = (counts + G - 1) // G * G
    pends = jnp.cumsum(pcounts)
    pstarts = pends - pcounts
    dest = (pstarts[flat_e] + rank).astype(I32)
    n_rows = (A + n_experts * (G - 1)) // G * G
    p_ids = jnp.arange(n_rows, dtype=I32)
    seg = jnp.minimum(jnp.sum((p_ids[:, None] >= pends[None, :]).astype(I32), axis=1), n_experts - 1)
    pad_ids = A + p_ids - jnp.cumsum(counts)[seg]
    row_asg = pad_ids.astype(I32).at[dest].set(jnp.arange(A, dtype=I32))
    row_asg = jnp.concatenate([row_asg, jnp.zeros((SUP,), I32)])

    n_sup = (A + n_experts * (G - 1) + n_experts * (SUP - G)) // SUP + 1
    nsup_e = (pcounts + SUP - 1) // SUP
    sup_ends = jnp.cumsum(nsup_e)
    total = sup_ends[-1]
    s_ids = jnp.arange(n_sup, dtype=I32)
    s_eff = jnp.minimum(s_ids, total - 1)
    e_of = jnp.sum((s_eff[:, None] >= sup_ends[None, :]).astype(I32), axis=1)
    local = s_eff - (sup_ends - nsup_e)[e_of]
    active = s_ids < total
    blk0 = jnp.where(active, (pstarts[e_of] + local * SUP) // G, 0)
    nsub = jnp.where(active, jnp.minimum(SUP // G, (pcounts[e_of] - local * SUP) // G), 0)
    order = jnp.argsort(-nsub, stable=True)
    e_sorted = e_of[order]
    sup_e = jnp.where(active, e_sorted, e_sorted[total - 1]).astype(I32)
    sup_blk = jnp.concatenate([blk0[order], pends[-1:] // G]).astype(I32)
    sup_nsub = nsub[order].astype(I32)
    return row_asg, n_rows, sup_e, sup_blk, sup_nsub


def _rope_tables(seq):
    half = HEAD_DIM // 2
    inv_freq = 1.0 / (ROPE_THETA ** (jnp.arange(0, HEAD_DIM, 2, dtype=F32) / HEAD_DIM))
    ang = jnp.arange(seq, dtype=F32)[:, None] * inv_freq[None, :]
    cos, sin = jnp.cos(ang), jnp.sin(ang)
    assert cos.shape[1] == half
    return jnp.concatenate([cos, cos], axis=1), jnp.concatenate([-sin, sin], axis=1)


def kernel(x, attn_norm_w, w_in, gla_gate_w2, gla_gate_b, gla_norm_w, w_out, ffn_norm_w,
           router_w, router_b, w_gate_up, b_gate_up, w_down, b_down, final_norm_w):
    B, S, D = x.shape
    T = B * S
    assert w_in.shape[0] == 1, "single-layer stack: the final norm is fused into combine"
    n_experts = router_w.shape[-1]
    n_main = w_in.shape[-1] - GLA_GATE_RANK
    cosf, sinf = _rope_tables(S)
    x2d = x.reshape(T, D)
    w_main = w_in[0, :, :n_main].astype(BF16)
    w_gz = jnp.pad(w_in[0, :, n_main:], ((0, 0), (0, LANES - GLA_GATE_RANK))).astype(BF16)
    w2_pad = jnp.pad(gla_gate_w2[0], ((0, LANES - GLA_GATE_RANK), (0, 0)))
    proj, log_a = _in_proj(x2d, attn_norm_w[0][None, :], w_main, w_gz, w2_pad,
                           gla_gate_b[0][None, :], cosf, sinf, S)
    moba_o = _moba(proj, B, S)
    gla_o = _gla(proj, log_a, gla_norm_w[0][None, :], B, S)
    rw_t = jnp.pad(router_w[0].T, ((0, LANES - n_experts), (0, 0)))
    rw_hi = rw_t.astype(BF16)
    rw_lo = (rw_t - rw_hi.astype(F32)).astype(BF16)
    rb_col = jnp.pad(router_b[0], (0, LANES - n_experts), constant_values=NEG)
    rb_tile = jnp.broadcast_to(rb_col[:, None], (LANES, LANES))
    x1, h2p, top_idx, gates = _out_proj(moba_o, gla_o, x2d, w_out[0].astype(BF16),
                                        ffn_norm_w[0][None, :], rw_hi, rw_lo, rb_tile)
    row_asg, n_rows, sup_e, sup_blk, sup_nsub = _route(top_idx[:, :TOP_K], n_experts)
    y_slots = _moe(sup_e, sup_blk, sup_nsub, row_asg, h2p, w_gate_up[0], b_gate_up[0],
                   w_down[0], b_down[0], n_rows)
    out = _combine(y_slots, x1, gates, final_norm_w[None, :])
    return out.reshape(B, S, D)
```

```python
import functools

import jax
import jax.numpy as jnp
from jax import lax
from jax.experimental import pallas as pl
from jax.experimental.pallas import tpu as pltpu

F32 = jnp.float32
BF16 = jnp.bfloat16
I32 = jnp.int32
U32 = jnp.uint32
HIGHEST = lax.Precision.HIGHEST

HEAD_DIM = 128
MOBA_HEADS = 8
MOBA_BLOCK = 256
MOBA_TOPK = 3
ROPE_THETA = 10000.0
GLA_HEADS = 4
GLA_DK = 128
GLA_DV = 256
GLA_GATE_RANK = 16
GLA_GATE_NORMALIZER = 16.0
TOP_K = 4
SWIGLU_ALPHA = 1.702
SWIGLU_LIMIT = 7.0
NORM_EPS = 1e-5

LANES = 128
VMEM_LIMIT = 58 * 1024 * 1024
NEG = -1e30

IN_PROJ_TM = 1024
PROJ_TM = 512
PROJ_TN = 1024
GLA_CHUNK = 64
GLA_GROUP = 8
GLA_HEADS_PER_STEP = 2
MOE_SUB = 256
MOE_SUP = 2048
MOE_TF = 256
MOE_DMA_UNROLL = 8
COMB_TM = 256


def _dot(a, b, **kw):
    return jnp.dot(a, b, preferred_element_type=F32, **kw)


def _dot_nt(a, b, **kw):
    return lax.dot_general(a, b, (((1,), (1,)), ((), ())), preferred_element_type=F32, **kw)


def _dot_tn(a, b, **kw):
    return lax.dot_general(a, b, (((0,), (0,)), ((), ())), preferred_element_type=F32, **kw)


def _rms(x, w):
    return x * lax.rsqrt(jnp.mean(x * x, axis=-1, keepdims=True) + NORM_EPS) * w


def _in_proj_kernel(x_ref, nw_ref, w_ref, wz_ref, w2_ref, gb_ref, cos_ref, sin_ref,
                    proj_ref, la_ref, h_scr, *, n_main, n_rope):
    j = pl.program_id(1)

    @pl.when(j == 0)
    def _():
        h_scr[...] = _rms(x_ref[...], nw_ref[...]).astype(BF16)

    @pl.when(j < n_rope)
    def _():
        acc = _dot(h_scr[...], w_ref[...])
        cosf = cos_ref[...]
        sinf = sin_ref[...]
        for hd in range(acc.shape[1] // HEAD_DIM):
            t = acc[:, hd * HEAD_DIM:(hd + 1) * HEAD_DIM]
            r = t * cosf + pltpu.roll(t, HEAD_DIM // 2, axis=1) * sinf
            proj_ref[:, hd * HEAD_DIM:(hd + 1) * HEAD_DIM] = r.astype(BF16)

    @pl.when((j >= n_rope) & (j < n_main))
    def _():
        proj_ref[...] = _dot(h_scr[...], w_ref[...]).astype(BF16)

    @pl.when(j == n_main)
    def _():
        gz = _dot(h_scr[...], wz_ref[...])
        logits = _dot(gz, w2_ref[...], precision=HIGHEST) + gb_ref[...]
        logsig = jnp.minimum(logits, 0.0) - jnp.log1p(jnp.exp(-jnp.abs(logits)))
        la_ref[...] = logsig * (1.0 / GLA_GATE_NORMALIZER)


def _in_proj(x2d, norm_w, w_main, w_gz, w2_pad, gate_b, cosf, sinf, seq):
    T, D = x2d.shape
    n_cols = w_main.shape[1]
    tm, tn = IN_PROJ_TM, PROJ_TN
    n_main = n_cols // tn
    n_rope = (2 * MOBA_HEADS * HEAD_DIM) // tn
    kw = w2_pad.shape[1]
    s_tiles = seq // tm
    last = n_main - 1
    return pl.pallas_call(
        functools.partial(_in_proj_kernel, n_main=n_main, n_rope=n_rope),
        grid=(T // tm, n_main + 1),
        in_specs=[
            pl.BlockSpec((tm, D), lambda i, j: (i, 0)),
            pl.BlockSpec((1, D), lambda i, j: (0, 0)),
            pl.BlockSpec((D, tn), lambda i, j: (0, jnp.minimum(j, last))),
            pl.BlockSpec((D, LANES), lambda i, j: (0, 0)),
            pl.BlockSpec((LANES, kw), lambda i, j: (0, 0)),
            pl.BlockSpec((1, kw), lambda i, j: (0, 0)),
            pl.BlockSpec((tm, HEAD_DIM), lambda i, j: (i % s_tiles, 0)),
            pl.BlockSpec((tm, HEAD_DIM), lambda i, j: (i % s_tiles, 0)),
        ],
        out_specs=[
            pl.BlockSpec((tm, tn), lambda i, j: (i, jnp.minimum(j, last))),
            pl.BlockSpec((tm, kw), lambda i, j: (i, 0)),
        ],
        out_shape=[
            jax.ShapeDtypeStruct((T, n_cols), BF16),
            jax.ShapeDtypeStruct((T, kw), F32),
        ],
        scratch_shapes=[pltpu.VMEM((tm, D), BF16)],
        compiler_params=pltpu.CompilerParams(
            dimension_semantics=("parallel", "arbitrary"), vmem_limit_bytes=VMEM_LIMIT),
        name="in_proj",
    )(x2d, norm_w, w_main, w_gz, w2_pad, gate_b, cosf, sinf)


MOBA_ONES_ROWS = 16


def _moba_select(q_ref, kmean_scr, qs_scr, which, iq, *, nb, blk, scale):
    hd = HEAD_DIM
    qf = q_ref[...].astype(F32)
    gate = _dot_nt(kmean_scr[...], qf, precision=HIGHEST)
    n_iota = lax.broadcasted_iota(I32, (nb, blk), 0)
    rank = jnp.zeros((nb, blk), I32)
    for m in range(nb):
        gm = gate[m:m + 1, :]
        beats = (gm > gate) | ((gm == gate) & (m < n_iota))
        rank = rank + jnp.where(beats & (m < iq), 1, 0)
    keep = ((n_iota < iq) & (rank < MOBA_TOPK)) | (n_iota == iq)
    maskv = jnp.where(keep, 0.0, NEG)
    maskv = jnp.concatenate([maskv, jnp.zeros((hd - nb, blk), F32)], axis=0)
    qs_scr[which, :, :hd] = (qf * scale).astype(BF16)
    qs_scr[which, :, hd:] = maskv.T.astype(BF16)


def _moba_kernel(qa_ref, qb_ref, k_ref, v_ref, oa_ref, ob_ref,
                 kmean_scr, kaug_scr, vt_scr, qs_scr, s_scr, acc_scr, *, nb, blk, scale):
    p = pl.program_id(2)
    hd = HEAD_DIM
    sub = 8

    @pl.when(p == 0)
    def _():
        col = lax.broadcasted_iota(I32, (blk, hd), 1)
        ones = jnp.ones((MOBA_ONES_ROWS, blk), BF16)
        for n in range(nb):
            rows = slice(n * blk, (n + 1) * blk)
            kb = k_ref[rows, :]
            kmean_scr[n:n + 1, :] = jnp.sum(kb.astype(F32), axis=0, keepdims=True) * (1.0 / blk)
            kaug_scr[rows, :hd] = kb
            kaug_scr[rows, hd:] = jnp.where(col == n, 1.0, 0.0).astype(BF16)
            vt_scr[n, :hd, :] = v_ref[rows, :].astype(F32).T.astype(BF16)
            vt_scr[n, hd:, :] = ones

    ia = p
    ib = nb - 1 - p
    _moba_select(qa_ref, kmean_scr, qs_scr, 0, ia, nb=nb, blk=blk, scale=scale)
    _moba_select(qb_ref, kmean_scr, qs_scr, 1, ib, nb=nb, blk=blk, scale=scale)

    def fold_max(x):
        return jnp.max(x.reshape(blk // sub, sub, blk), axis=0)

    def key_rows(jb):
        return kaug_scr[pl.ds(pl.multiple_of(jb * blk, blk), blk), :]

    kpos = lax.broadcasted_iota(I32, (blk, blk), 0)
    qpos = lax.broadcasted_iota(I32, (blk, blk), 1)
    causal = jnp.where(kpos <= qpos, 0.0, NEG)

    s_own_a = _dot_nt(key_rows(ia), qs_scr[0]) + causal
    s_own_b = _dot_nt(key_rows(ib), qs_scr[1]) + causal
    mx_a = fold_max(s_own_a)
    mx_b = fold_max(s_own_b)
    trips = []
    for t in range(nb - 1):
        is_a = t < p
        which = jnp.where(is_a, 0, 1)
        jb = jnp.where(is_a, t, t - p)
        trips.append((is_a, which, jb))
        s = _dot_nt(key_rows(jb), qs_scr[which])
        s_scr[t] = s
        f = fold_max(s)
        mx_a = jnp.maximum(mx_a, jnp.where(is_a, f, NEG))
        mx_b = jnp.maximum(mx_b, jnp.where(is_a, NEG, f))
    m_a = jnp.max(mx_a, axis=0, keepdims=True)
    m_b = jnp.max(mx_b, axis=0, keepdims=True)

    acc_scr[0] = _dot(vt_scr[ia], jnp.exp(s_own_a - m_a).astype(BF16))
    acc_scr[1] = _dot(vt_scr[ib], jnp.exp(s_own_b - m_b).astype(BF16))
    for t, (is_a, which, jb) in enumerate(trips):
        pr = jnp.exp(s_scr[t] - jnp.where(is_a, m_a, m_b))
        acc_scr[which] += _dot(vt_scr[jb], pr.astype(BF16))
    for which, o_ref in ((0, oa_ref), (1, ob_ref)):
        acc = acc_scr[which]
        o_ref[...] = (acc[:hd] / acc[hd:hd + 1]).T.astype(BF16)


def _moba(proj, batch, seq):
    blk = MOBA_BLOCK
    nb = seq // blk
    assert nb % 2 == 0 and nb <= HEAD_DIM
    half = nb // 2
    H = MOBA_HEADS
    hd = HEAD_DIM
    width = H * hd
    lo, hi = pl.pallas_call(
        functools.partial(_moba_kernel, nb=nb, blk=blk, scale=hd ** -0.5),
        grid=(batch, H, half),
        in_specs=[
            pl.BlockSpec((blk, hd), lambda b, h, p: (b * nb + p, h)),
            pl.BlockSpec((blk, hd), lambda b, h, p: (b * nb + nb - 1 - p, h)),
            pl.BlockSpec((seq, hd), lambda b, h, p: (b, H + h)),
            pl.BlockSpec((seq, hd), lambda b, h, p: (b, 2 * H + h)),
        ],
        out_specs=[
            pl.BlockSpec((blk, hd), lambda b, h, p: (b * half + p, h)),
            pl.BlockSpec((blk, hd), lambda b, h, p: (b * half + half - 1 - p, h)),
        ],
        out_shape=[jax.ShapeDtypeStruct((batch * half * blk, width), BF16)] * 2,
        scratch_shapes=[
            pltpu.VMEM((nb, hd), F32),
            pltpu.VMEM((seq, 2 * hd), BF16),
            pltpu.VMEM((nb, hd + MOBA_ONES_ROWS, blk), BF16),
            pltpu.VMEM((2, blk, 2 * hd), BF16),
            pltpu.VMEM((nb - 1, blk, blk), F32),
            pltpu.VMEM((2, hd + MOBA_ONES_ROWS, blk), F32),
        ],
        compiler_params=pltpu.CompilerParams(
            dimension_semantics=("parallel", "parallel", "arbitrary"), vmem_limit_bytes=VMEM_LIMIT),
        name="moba",
    )(proj, proj, proj, proj)
    shape3 = (batch, half * blk, width)
    return jnp.concatenate([lo.reshape(shape3), hi.reshape(shape3)], axis=1).reshape(batch * seq, width)


def _gla_levels(c):
    out = []
    m = c // 2
    while m >= 1:
        out.append(m)
        m //= 2
    return out


def _gla_kernel(q_ref, k_ref, v_ref, g_ref, la_ref, nw_ref, o_ref, st_scr, *,
                chunk, group, heads, scale):
    C = chunk
    levels = _gla_levels(C)

    @pl.when(pl.program_id(2) == 0)
    def _():
        st_scr[...] = jnp.zeros_like(st_scr)

    t_io = lax.broadcasted_iota(I32, (C, C), 0)
    s_io = lax.broadcasted_iota(I32, (C, C), 1)
    sel_rows = [jnp.where(s_io <= t_io, 1.0, 0.0)]
    masks = []
    for m in levels:
        blk_t = t_io // (2 * m)
        bound = blk_t * (2 * m) + (m - 1)
        sel_rows.append(jnp.where(s_io <= bound, 1.0, 0.0))
        masks.append((blk_t == s_io // (2 * m)) & (t_io % (2 * m) >= m) & (s_io % (2 * m) < m))
    cum_mat = jnp.concatenate(sel_rows, axis=0)
    diag = t_io == s_io
    nw = nw_ref[...]

    dk, dv = GLA_DK, GLA_DV
    for c in range(group):
        rows = slice(c * C, (c + 1) * C)
        for hh in range(heads):
            kc = slice(hh * dk, (hh + 1) * dk)
            vc = slice(hh * dv, (hh + 1) * dv)
            la = la_ref[rows, kc]
            cums = _dot(cum_mat, la, precision=HIGHEST)
            b = cums[:C]
            q = q_ref[rows, kc].astype(F32) * scale
            k = k_ref[rows, kc].astype(F32)
            v = v_ref[rows, vc]
            a = jnp.where(diag, _dot_nt(q.astype(BF16), k.astype(BF16)), 0.0)
            for li in range(len(levels)):
                e = jnp.exp(-jnp.abs(b - cums[(li + 1) * C:(li + 2) * C]))
                a = a + jnp.where(masks[li], _dot_nt((q * e).astype(BF16), (k * e).astype(BF16)), 0.0)
            st = st_scr[hh]
            o = _dot(a.astype(BF16), v) + _dot_nt((q * jnp.exp(b)).astype(BF16), st.astype(BF16))
            b_last = b[C - 1:C, :]
            ke = (k * jnp.exp(b_last - b)).astype(BF16)
            st_scr[hh] = st * jnp.exp(b_last) + _dot_tn(v, ke)
            gg = g_ref[rows, vc].astype(F32)
            o_ref[rows, vc] = (_rms(o, nw) * (gg * jax.nn.sigmoid(gg))).astype(BF16)


def _gla(proj, la, norm_w, batch, seq):
    T = proj.shape[0]
    Hg = GLA_HEADS
    rows = GLA_CHUNK * GLA_GROUP
    ng = seq // rows
    hs = GLA_HEADS_PER_STEP
    wk, wv = hs * GLA_DK, hs * GLA_DV
    q0 = (3 * MOBA_HEADS * HEAD_DIM) // wk
    k0 = q0 + Hg // hs
    v0 = (3 * MOBA_HEADS * HEAD_DIM + 2 * Hg * GLA_DK) // wv
    g0 = v0 + Hg // hs
    return pl.pallas_call(
        functools.partial(_gla_kernel, chunk=GLA_CHUNK, group=GLA_GROUP, heads=hs,
                          scale=GLA_DK ** -0.5),
        grid=(batch, Hg // hs, ng),
        in_specs=[
            pl.BlockSpec((rows, wk), lambda b, h, g: (b * ng + g, q0 + h)),
            pl.BlockSpec((rows, wk), lambda b, h, g: (b * ng + g, k0 + h)),
            pl.BlockSpec((rows, wv), lambda b, h, g: (b * ng + g, v0 + h)),
            pl.BlockSpec((rows, wv), lambda b, h, g: (b * ng + g, g0 + h)),
            pl.BlockSpec((rows, wk), lambda b, h, g: (b * ng + g, h)),
            pl.BlockSpec((1, GLA_DV), lambda b, h, g: (0, 0)),
        ],
        out_specs=pl.BlockSpec((rows, wv), lambda b, h, g: (b * ng + g, h)),
        out_shape=jax.ShapeDtypeStruct((T, Hg * GLA_DV), BF16),
        scratch_shapes=[pltpu.VMEM((hs, GLA_DV, GLA_DK), F32)],
        compiler_params=pltpu.CompilerParams(
            dimension_semantics=("parallel", "parallel", "arbitrary"), vmem_limit_bytes=VMEM_LIMIT),
        name="gla",
    )(proj, proj, proj, proj, la, norm_w)


def _split_bf16(x):
    hi = x.astype(BF16)
    return hi, (x - hi.astype(F32)).astype(BF16)


def _pack_bf16_pairs(x):
    n = x.shape[1] // 2
    lo = lax.bitcast_convert_type(x[:, :n].astype(BF16).astype(F32), U32)
    hi = lax.bitcast_convert_type(x[:, n:].astype(BF16).astype(F32), U32)
    return (lo >> 16) | (hi & jnp.uint32(0xFFFF0000))


def _unpack_bf16_pairs(p):
    lo = lax.bitcast_convert_type(p << 16, F32).astype(BF16)
    hi = lax.bitcast_convert_type(p & jnp.uint32(0xFFFF0000), F32).astype(BF16)
    return lo, hi


def _out_proj_kernel(mo_ref, gl_ref, x_ref, wo_ref, nw_ref, rwh_ref, rwl_ref, rb_ref,
                     x1_ref, h2p_ref, idx_ref, gate_ref):
    wm = mo_ref.shape[1]
    x1 = x_ref[...] + _dot(mo_ref[...], wo_ref[:wm, :]) + _dot(gl_ref[...], wo_ref[wm:, :])
    x1_ref[...] = x1
    h2 = _rms(x1, nw_ref[...])
    h2p_ref[...] = _pack_bf16_pairs(h2)
    h_hi, h_lo = _split_bf16(h2)
    rw_hi = rwh_ref[...]
    logits = _dot_nt(rw_hi, h_hi) + (_dot_nt(rwl_ref[...], h_hi) + _dot_nt(rw_hi, h_lo))
    tm = logits.shape[1]
    logits = logits + jnp.concatenate([rb_ref[...]] * (tm // LANES), axis=1)
    row = lax.broadcasted_iota(I32, logits.shape, 0)
    vals, idxs = [], []
    for _ in range(TOP_K):
        mx = jnp.max(logits, axis=0, keepdims=True)
        ix = jnp.min(jnp.where(logits == mx, row, LANES), axis=0, keepdims=True)
        vals.append(mx)
        idxs.append(ix)
        logits = jnp.where(row == ix, -jnp.inf, logits)
    exps = [jnp.exp(v - vals[0]) for v in vals]
    denom = exps[0]
    for e in exps[1:]:
        denom = denom + e
    idx_t = jnp.zeros(logits.shape, I32)
    gate_t = jnp.zeros(logits.shape, F32)
    for r in range(TOP_K):
        idx_t = jnp.where(row == r, idxs[r], idx_t)
        gate_t = jnp.where(row == r, exps[r] / denom, gate_t)
    idx_ref[...] = idx_t.astype(F32).T.astype(I32)
    gate_ref[...] = gate_t.T


def _out_proj(moba_o, gla_o, x2d, w_out, norm_w, rw_hi, rw_lo, rb_tile):
    T, D = x2d.shape
    tm = PROJ_TM
    wm, wg = moba_o.shape[1], gla_o.shape[1]
    return pl.pallas_call(
        _out_proj_kernel,
        grid=(T // tm,),
        in_specs=[
            pl.BlockSpec((tm, wm), lambda i: (i, 0)),
            pl.BlockSpec((tm, wg), lambda i: (i, 0)),
            pl.BlockSpec((tm, D), lambda i: (i, 0)),
            pl.BlockSpec((wm + wg, D), lambda i: (0, 0)),
            pl.BlockSpec((1, D), lambda i: (0, 0)),
            pl.BlockSpec((LANES, D), lambda i: (0, 0)),
            pl.BlockSpec((LANES, D), lambda i: (0, 0)),
            pl.BlockSpec((LANES, LANES), lambda i: (0, 0)),
        ],
        out_specs=[
            pl.BlockSpec((tm, D), lambda i: (i, 0)),
            pl.BlockSpec((tm, D // 2), lambda i: (i, 0)),
            pl.BlockSpec((tm, LANES), lambda i: (i, 0)),
            pl.BlockSpec((tm, LANES), lambda i: (i, 0)),
        ],
        out_shape=[
            jax.ShapeDtypeStruct((T, D), F32),
            jax.ShapeDtypeStruct((T, D // 2), U32),
            jax.ShapeDtypeStruct((T, LANES), I32),
            jax.ShapeDtypeStruct((T, LANES), F32),
        ],
        compiler_params=pltpu.CompilerParams(
            dimension_semantics=("parallel",), vmem_limit_bytes=VMEM_LIMIT),
        name="out_proj",
    )(moba_o, gla_o, x2d, w_out, norm_w, rw_hi, rw_lo, rb_tile)


def _moe_row_in(h_hbm, xbuf, sem, slot, tok, r):
    return pltpu.make_async_copy(h_hbm.at[pl.ds(tok, 1), :], xbuf.at[slot, pl.ds(r, 1), :],
                                 sem.at[slot])


def _moe_row_out(acc, y_hbm, sem, r, dst_row):
    return pltpu.make_async_copy(acc.at[pl.ds(r, 1), :], y_hbm.at[pl.ds(dst_row, 1), :], sem)


def _moe_blk_out(acc, y_hbm, sem, blk):
    dst = y_hbm.at[pl.ds(pl.multiple_of(blk * MOE_SUB, MOE_SUB), MOE_SUB), :]
    return pltpu.make_async_copy(acc.at[pl.ds(0, MOE_SUB), :], dst, sem)


def _moe_kernel(se_ref, rs_ref, ns_ref, asg_ref,
                h_hbm, wg_ref, wu_ref, wd_ref, bg_ref, bu_ref, bd_ref, y_hbm,
                xbuf, acc, wgb, wub, wdb, gsem, osem, *, n_tok, nf):
    s = pl.program_id(0)
    j = pl.program_id(1)
    n_grid = pl.num_programs(0)
    nsub = ns_ref[s]
    base = rs_ref[s] * MOE_SUB
    slot = s & 1
    per_trip = MOE_SUB // nf
    half = wgb.shape[0] // 2

    def src_token(p):
        a = asg_ref[p]
        k = (a >= n_tok).astype(I32)
        for kk in range(2, TOP_K):
            k = k + (a >= kk * n_tok).astype(I32)
        return jnp.minimum(a - k * n_tok, n_tok - 1)

    def wait_rows(n_sub_tiles):
        def wait(gidx, c):
            for u in range(MOE_DMA_UNROLL):
                _moe_row_in(h_hbm, xbuf, gsem, slot, 0, gidx * MOE_DMA_UNROLL + u).wait()
            return c

        lax.fori_loop(0, n_sub_tiles * (MOE_SUB // MOE_DMA_UNROLL), wait, 0)

    @pl.when(j == 0)
    def _():
        @pl.when(s == 0)
        def _():
            def issue(gidx, c):
                for u in range(MOE_DMA_UNROLL):
                    r = gidx * MOE_DMA_UNROLL + u
                    _moe_row_in(h_hbm, xbuf, gsem, slot, src_token(base + r), r).start()
                return c

            lax.fori_loop(0, nsub * (MOE_SUB // MOE_DMA_UNROLL), issue, 0)

        wait_rows(jnp.where(s == 0, nsub, ns_ref[jnp.maximum(s - 1, 0)]))

    @pl.when(nsub > 0)
    def _():
        @pl.when(j == 0)
        def _():
            bias_rows = jnp.broadcast_to(bd_ref[0], (MOE_SUB, acc.shape[1]))

            def init(r, c):
                acc[pl.ds(pl.multiple_of(r * MOE_SUB, MOE_SUB), MOE_SUB), :] = bias_rows
                return c

            lax.fori_loop(0, nsub, init, 0)

        wgb[...] = wg_ref[0].astype(BF16)
        wub[...] = wu_ref[0].astype(BF16)
        wdb[...] = wd_ref[0].astype(BF16)
        bg = bg_ref[0]
        bu = bu_ref[0]
        next_base = rs_ref[s + 1] * MOE_SUB

        def sub_tile(r, emit):
            off = pl.multiple_of(r * MOE_SUB, MOE_SUB)
            xl, xh = _unpack_bf16_pairs(xbuf[slot, pl.ds(off, MOE_SUB), :])
            g = _dot(xl, wgb[:half, :]) + _dot(xh, wgb[half:, :]) + bg
            u = _dot(xl, wub[:half, :]) + _dot(xh, wub[half:, :]) + bu
            g = jnp.minimum(g, SWIGLU_LIMIT)
            u = jnp.clip(u, -SWIGLU_LIMIT, SWIGLU_LIMIT)
            act = (u + 1.0) * (g * jax.nn.sigmoid(SWIGLU_ALPHA * g))
            acc[pl.ds(off, MOE_SUB), :] += _dot(act.astype(BF16), wdb[...])
            g0 = pl.multiple_of((j * nsub + r) * per_trip, per_trip)
            x_next = xbuf.at[1 - slot, pl.ds(g0, per_trip), :]
            for v in range(per_trip):
                tok = src_token(next_base + g0 + v)
                pltpu.make_async_copy(h_hbm.at[pl.ds(tok, 1), :], x_next.at[pl.ds(v, 1), :],
                                      gsem.at[1 - slot]).start()
            if emit:
                acc_rows = acc.at[pl.ds(off, MOE_SUB), :]
                for v in range(MOE_SUB):
                    _moe_row_out(acc_rows, y_hbm, osem, v, asg_ref[base + off + v]).start()

        def sweep(emit):
            def pair(t, c):
                sub_tile(2 * t, emit)
                sub_tile(2 * t + 1, emit)
                return c

            lax.fori_loop(0, nsub >> 1, pair, 0)

            @pl.when((nsub & 1) == 1)
            def _():
                sub_tile(nsub - 1, emit)

        @pl.when(j < nf - 1)
        def _():
            sweep(False)

        @pl.when(j == nf - 1)
        def _():
            sweep(True)

            def done(gidx, c):
                for u in range(MOE_DMA_UNROLL):
                    _moe_row_out(acc, y_hbm, osem, 0, 0).wait()
                return c

            lax.fori_loop(0, nsub * (MOE_SUB // MOE_DMA_UNROLL), done, 0)

    @pl.when((s == n_grid - 1) & (j == nf - 1))
    def _():
        tail0 = rs_ref[n_grid]
        n_blk = y_hbm.shape[0] // MOE_SUB
        acc[0:MOE_SUB, :] = jnp.zeros((MOE_SUB, acc.shape[1]), F32)

        def put(b, c):
            _moe_blk_out(acc, y_hbm, osem, b).start()
            return c

        lax.fori_loop(tail0, n_blk, put, 0)

        def done(b, c):
            _moe_blk_out(acc, y_hbm, osem, b).wait()
            return c

        lax.fori_loop(tail0, n_blk, done, 0)


def _moe(sup_e, sup_blk, sup_nsub, row_asg, h2p, w_gu, b_gu, w_dn, b_dn, n_rows):
    T, DP = h2p.shape
    D = 2 * DP
    E, _, F2 = w_gu.shape
    F = F2 // 2
    tf = MOE_TF
    nf = F // tf
    assert MOE_SUB % nf == 0
    n_sup = sup_e.shape[0]

    def wmap(col0):
        def f(s, j, se, rs, ns, tok):
            return (se[s], 0, col0 + jnp.where(ns[s] > 0, j, nf - 1))
        return f

    def dmap(s, j, se, rs, ns, tok):
        return (se[s], jnp.where(ns[s] > 0, j, nf - 1), 0)

    def bdmap(s, j, se, rs, ns, tok):
        return (se[s], 0, 0)

    grid_spec = pltpu.PrefetchScalarGridSpec(
        num_scalar_prefetch=4,
        grid=(n_sup, nf),
        in_specs=[
            pl.BlockSpec(memory_space=pl.ANY),
            pl.BlockSpec((1, D, tf), wmap(0)),
            pl.BlockSpec((1, D, tf), wmap(nf)),
            pl.BlockSpec((1, tf, D), dmap),
            pl.BlockSpec((1, 1, tf), wmap(0)),
            pl.BlockSpec((1, 1, tf), wmap(nf)),
            pl.BlockSpec((1, 1, D), bdmap),
        ],
        out_specs=pl.BlockSpec(memory_space=pl.ANY),
        scratch_shapes=[
            pltpu.VMEM((2, MOE_SUP, DP), U32),
            pltpu.VMEM((MOE_SUP, D), F32),
            pltpu.VMEM((D, tf), BF16),
            pltpu.VMEM((D, tf), BF16),
            pltpu.VMEM((tf, D), BF16),
            pltpu.SemaphoreType.DMA((2,)),
            pltpu.SemaphoreType.DMA,
        ],
    )
    return pl.pallas_call(
        functools.partial(_moe_kernel, n_tok=T, nf=nf),
        grid_spec=grid_spec,
        out_shape=jax.ShapeDtypeStruct((n_rows, D), F32),
        compiler_params=pltpu.CompilerParams(
            dimension_semantics=("arbitrary", "arbitrary"), vmem_limit_bytes=VMEM_LIMIT),
        name="moe",
    )(sup_e, sup_blk, sup_nsub, row_asg, h2p, w_gu, w_gu, w_dn,
      b_gu.reshape(E, 1, F2), b_gu.reshape(E, 1, F2), b_dn.reshape(E, 1, D))


def _combine_kernel(*refs):
    y_refs = refs[:TOP_K]
    x1_ref, gate_ref, nw_ref, o_ref = refs[TOP_K:]
    gates = gate_ref[...]
    x2 = x1_ref[...]
    for kk in range(TOP_K):
        x2 = x2 + gates[:, kk:kk + 1] * y_refs[kk][...]
    o_ref[...] = _rms(x2, nw_ref[...])


def _combine(y_slots, x1, gates, norm_w):
    T, D = x1.shape
    tm = COMB_TM
    tiles = T // tm

    def plane(kk):
        return pl.BlockSpec((tm, D), lambda i: (kk * tiles + i, 0))

    return pl.pallas_call(
        _combine_kernel,
        grid=(tiles,),
        in_specs=[plane(kk) for kk in range(TOP_K)] + [
            pl.BlockSpec((tm, D), lambda i: (i, 0)),
            pl.BlockSpec((tm, LANES), lambda i: (i, 0)),
            pl.BlockSpec((1, D), lambda i: (0, 0)),
        ],
        out_specs=pl.BlockSpec((tm, D), lambda i: (i, 0)),
        out_shape=jax.ShapeDtypeStruct((T, D), F32),
        compiler_params=pltpu.CompilerParams(
            dimension_semantics=("parallel",), vmem_limit_bytes=VMEM_LIMIT),
        name="combine",
    )(*([y_slots] * TOP_K), x1, gates, norm_w)


def _route(top_idx, n_experts):
    T = top_idx.shape[0]
    A = T * TOP_K
    G, SUP = MOE_SUB, MOE_SUP
    flat_e = top_idx.reshape(A)
    onehot = (flat_e[:, None] == jnp.arange(n_experts, dtype=I32)[None, :]).astype(I32)
    csum = jnp.cumsum(onehot, axis=0)
    counts = csum[-1]
    rank = jnp.sum((csum - 1) * onehot, axis=1)
    pcounts = (counts + G - 1) // G * G
    pends = jnp.cumsum(pcounts)
    pstarts = pends - pcounts
    dest = (pstarts[flat_e] + rank).astype(I32)
    n_rows = (A + n_experts * (G - 1)) // G * G
    p_ids = jnp.arange(n_rows, dtype=I32)
    seg = jnp.minimum(jnp.sum((p_ids[:, None] >= pends[None, :]).astype(I32), axis=1), n_experts - 1)
    pad_ids = A + p_ids - jnp.cumsum(counts)[seg]
    a_ids = jnp.arange(A, dtype=I32)
    slot_ids = (a_ids % TOP_K) * T + a_ids // TOP_K
    row_asg = pad_ids.astype(I32).at[dest].set(slot_ids)
    row_asg = jnp.concatenate([row_asg, jnp.zeros((SUP,), I32)])

    n_sup = (A + n_experts * (G - 1) + n_experts * (SUP - G)) // SUP + 1
    sub_e = pcounts // G
    nsup_e = (pcounts + SUP - 1) // SUP
    sup_ends = jnp.cumsum(nsup_e)
    total = sup_ends[-1]
    s_ids = jnp.arange(n_sup, dtype=I32)
    s_eff = jnp.minimum(s_ids, total - 1)
    e_of = jnp.sum((s_eff[:, None] >= sup_ends[None, :]).astype(I32), axis=1)
    local = s_eff - (sup_ends - nsup_e)[e_of]
    active = s_ids < total
    n_e = jnp.maximum(nsup_e[e_of], 1)
    small, extra = sub_e[e_of] // n_e, sub_e[e_of] % n_e
    blk0 = jnp.where(active, pstarts[e_of] // G + local * small + jnp.minimum(local, extra), 0)
    nsub = jnp.where(active, small + (local < extra).astype(I32), 0)
    order = jnp.argsort(-nsub, stable=True)
    e_sorted = e_of[order]
    sup_e = jnp.where(active, e_sorted, e_sorted[total - 1]).astype(I32)
    sup_blk = jnp.concatenate([blk0[order], pends[-1:] // G]).astype(I32)
    sup_nsub = nsub[order].astype(I32)
    return row_asg, n_rows, sup_e, sup_blk, sup_nsub


def _rope_tables(seq):
    half = HEAD_DIM // 2
    inv_freq = 1.0 / (ROPE_THETA ** (jnp.arange(0, HEAD_DIM, 2, dtype=F32) / HEAD_DIM))
    ang = jnp.arange(seq, dtype=F32)[:, None] * inv_freq[None, :]
    cos, sin = jnp.cos(ang), jnp.sin(ang)
    assert cos.shape[1] == half
    return jnp.concatenate([cos, cos], axis=1), jnp.concatenate([-sin, sin], axis=1)


def kernel(x, attn_norm_w, w_in, gla_gate_w2, gla_gate_b, gla_norm_w, w_out, ffn_norm_w,
           router_w, router_b, w_gate_up, b_gate_up, w_down, b_down, final_norm_w):
    B, S, D = x.shape
    T = B * S
    assert w_in.shape[0] == 1, "single-layer stack: the final norm is fused into combine"
    n_experts = router_w.shape[-1]
    n_main = w_in.shape[-1] - GLA_GATE_RANK
    cosf, sinf = _rope_tables(S)
    x2d = x.reshape(T, D)
    w_main = w_in[0, :, :n_main].astype(BF16)
    w_gz = jnp.pad(w_in[0, :, n_main:], ((0, 0), (0, LANES - GLA_GATE_RANK))).astype(BF16)
    w2_pad = jnp.pad(gla_gate_w2[0], ((0, LANES - GLA_GATE_RANK), (0, 0)))
    proj, log_a = _in_proj(x2d, attn_norm_w[0][None, :], w_main, w_gz, w2_pad,
                           gla_gate_b[0][None, :], cosf, sinf, S)
    moba_o = _moba(proj, B, S)
    gla_o = _gla(proj, log_a, gla_norm_w[0][None, :], B, S)
    rw_t = jnp.pad(router_w[0].T, ((0, LANES - n_experts), (0, 0)))
    rw_hi = rw_t.astype(BF16)
    rw_lo = (rw_t - rw_hi.astype(F32)).astype(BF16)
    rb_col = jnp.pad(router_b[0], (0, LANES - n_experts), constant_values=NEG)
    rb_tile = jnp.broadcast_to(rb_col[:, None], (LANES, LANES))
    x1, h2p, top_idx, gates = _out_proj(moba_o, gla_o, x2d, w_out[0].astype(BF16),
                                        ffn_norm_w[0][None, :], rw_hi, rw_lo, rb_tile)
    row_asg, n_rows, sup_e, sup_blk, sup_nsub = _route(top_idx[:, :TOP_K], n_experts)
    y_slots = _moe(sup_e, sup_blk, sup_nsub, row_asg, h2p, w_gate_up[0], b_gate_up[0],
                   w_down[0], b_down[0], n_rows)
    out = _combine(y_slots, x1, gates, final_norm_w[None, :])
    return out.reshape(B, S, D)
```

```python
import functools

import jax
import jax.numpy as jnp
from jax import lax
from jax.experimental import pallas as pl
from jax.experimental.pallas import tpu as pltpu

F32 = jnp.float32
BF16 = jnp.bfloat16
I32 = jnp.int32
U32 = jnp.uint32
HIGHEST = lax.Precision.HIGHEST

HEAD_DIM = 128
MOBA_HEADS = 8
MOBA_BLOCK = 256
MOBA_TOPK = 3
ROPE_THETA = 10000.0
GLA_HEADS = 4
GLA_DK = 128
GLA_DV = 256
GLA_GATE_RANK = 16
GLA_GATE_NORMALIZER = 16.0
TOP_K = 4
SWIGLU_ALPHA = 1.702
SWIGLU_LIMIT = 7.0
NORM_EPS = 1e-5

LANES = 128
VMEM_LIMIT = 58 * 1024 * 1024
NEG = -1e30

IN_PROJ_TM = 1024
PROJ_TM = 512
PROJ_TN = 1024
GLA_CHUNK = 64
GLA_GROUP = 8
GLA_HEADS_PER_STEP = 2
MOE_SUB = 256
MOE_SUP = 2048
MOE_TF = 256
MOE_DMA_UNROLL = 8
COMB_TM = 256


def _dot(a, b, **kw):
    return jnp.dot(a, b, preferred_element_type=F32, **kw)


def _dot_nt(a, b, **kw):
    return lax.dot_general(a, b, (((1,), (1,)), ((), ())), preferred_element_type=F32, **kw)


def _dot_tn(a, b, **kw):
    return lax.dot_general(a, b, (((0,), (0,)), ((), ())), preferred_element_type=F32, **kw)


def _rms(x, w):
    return x * lax.rsqrt(jnp.mean(x * x, axis=-1, keepdims=True) + NORM_EPS) * w


def _in_proj_kernel(x_ref, nw_ref, w_ref, wz_ref, w2_ref, gb_ref, cos_ref, sin_ref,
                    proj_ref, la_ref, h_scr, *, n_main, n_rope):
    j = pl.program_id(1)

    @pl.when(j == 0)
    def _():
        h_scr[...] = _rms(x_ref[...], nw_ref[...]).astype(BF16)

    @pl.when(j < n_rope)
    def _():
        acc = _dot(h_scr[...], w_ref[...])
        cosf = cos_ref[...]
        sinf = sin_ref[...]
        for hd in range(acc.shape[1] // HEAD_DIM):
            t = acc[:, hd * HEAD_DIM:(hd + 1) * HEAD_DIM]
            r = t * cosf + pltpu.roll(t, HEAD_DIM // 2, axis=1) * sinf
            proj_ref[:, hd * HEAD_DIM:(hd + 1) * HEAD_DIM] = r.astype(BF16)

    @pl.when((j >= n_rope) & (j < n_main))
    def _():
        proj_ref[...] = _dot(h_scr[...], w_ref[...]).astype(BF16)

    @pl.when(j == n_main)
    def _():
        gz = _dot(h_scr[...], wz_ref[...])
        logits = _dot(gz, w2_ref[...], precision=HIGHEST) + gb_ref[...]
        logsig = jnp.minimum(logits, 0.0) - jnp.log1p(jnp.exp(-jnp.abs(logits)))
        la_ref[...] = logsig * (1.0 / GLA_GATE_NORMALIZER)


def _in_proj(x2d, norm_w, w_main, w_gz, w2_pad, gate_b, cosf, sinf, seq):
    T, D = x2d.shape
    n_cols = w_main.shape[1]
    tm, tn = IN_PROJ_TM, PROJ_TN
    n_main = n_cols // tn
    n_rope = (2 * MOBA_HEADS * HEAD_DIM) // tn
    kw = w2_pad.shape[1]
    s_tiles = seq // tm
    last = n_main - 1
    return pl.pallas_call(
        functools.partial(_in_proj_kernel, n_main=n_main, n_rope=n_rope),
        grid=(T // tm, n_main + 1),
        in_specs=[
            pl.BlockSpec((tm, D), lambda i, j: (i, 0)),
            pl.BlockSpec((1, D), lambda i, j: (0, 0)),
            pl.BlockSpec((D, tn), lambda i, j: (0, jnp.minimum(j, last))),
            pl.BlockSpec((D, LANES), lambda i, j: (0, 0)),
            pl.BlockSpec((LANES, kw), lambda i, j: (0, 0)),
            pl.BlockSpec((1, kw), lambda i, j: (0, 0)),
            pl.BlockSpec((tm, HEAD_DIM), lambda i, j: (i % s_tiles, 0)),
            pl.BlockSpec((tm, HEAD_DIM), lambda i, j: (i % s_tiles, 0)),
        ],
        out_specs=[
            pl.BlockSpec((tm, tn), lambda i, j: (i, jnp.minimum(j, last))),
            pl.BlockSpec((tm, kw), lambda i, j: (i, 0)),
        ],
        out_shape=[
            jax.ShapeDtypeStruct((T, n_cols), BF16),
            jax.ShapeDtypeStruct((T, kw), F32),
        ],
        scratch_shapes=[pltpu.VMEM((tm, D), BF16)],
        compiler_params=pltpu.CompilerParams(
            dimension_semantics=("parallel", "arbitrary"), vmem_limit_bytes=VMEM_LIMIT),
        name="in_proj",
    )(x2d, norm_w, w_main, w_gz, w2_pad, gate_b, cosf, sinf)


MOBA_ONES_ROWS = 16


def _moba_select(q_ref, kmean_scr, qs_scr, which, iq, *, nb, blk, scale):
    hd = HEAD_DIM
    qf = q_ref[...].astype(F32)
    gate = _dot_nt(kmean_scr[...], qf, precision=HIGHEST)
    n_iota = lax.broadcasted_iota(I32, (nb, blk), 0)
    rank = jnp.zeros((nb, blk), I32)
    for m in range(nb):
        gm = gate[m:m + 1, :]
        beats = (gm > gate) | ((gm == gate) & (m < n_iota))
        rank = rank + jnp.where(beats & (m < iq), 1, 0)
    keep = ((n_iota < iq) & (rank < MOBA_TOPK)) | (n_iota == iq)
    maskv = jnp.where(keep, 0.0, NEG)
    maskv = jnp.concatenate([maskv, jnp.zeros((hd - nb, blk), F32)], axis=0)
    qs_scr[which, :, :hd] = (qf * scale).astype(BF16)
    qs_scr[which, :, hd:] = maskv.T.astype(BF16)


def _moba_kernel(qa_ref, qb_ref, k_ref, v_ref, oa_ref, ob_ref,
                 kmean_scr, kaug_scr, vt_scr, qs_scr, s_scr, acc_scr, *, nb, blk, scale):
    p = pl.program_id(2)
    hd = HEAD_DIM
    sub = 8

    @pl.when(p == 0)
    def _():
        col = lax.broadcasted_iota(I32, (blk, hd), 1)
        ones = jnp.ones((MOBA_ONES_ROWS, blk), BF16)
        for n in range(nb):
            rows = slice(n * blk, (n + 1) * blk)
            kb = k_ref[rows, :]
            kmean_scr[n:n + 1, :] = jnp.sum(kb.astype(F32), axis=0, keepdims=True) * (1.0 / blk)
            kaug_scr[rows, :hd] = kb
            kaug_scr[rows, hd:] = jnp.where(col == n, 1.0, 0.0).astype(BF16)
            vt_scr[n, :hd, :] = v_ref[rows, :].astype(F32).T.astype(BF16)
            vt_scr[n, hd:, :] = ones

    ia = p
    ib = nb - 1 - p
    _moba_select(qa_ref, kmean_scr, qs_scr, 0, ia, nb=nb, blk=blk, scale=scale)
    _moba_select(qb_ref, kmean_scr, qs_scr, 1, ib, nb=nb, blk=blk, scale=scale)

    def fold_max(x):
        return jnp.max(x.reshape(blk // sub, sub, blk), axis=0)

    def key_rows(jb):
        return kaug_scr[pl.ds(pl.multiple_of(jb * blk, blk), blk), :]

    kpos = lax.broadcasted_iota(I32, (blk, blk), 0)
    qpos = lax.broadcasted_iota(I32, (blk, blk), 1)
    causal = jnp.where(kpos <= qpos, 0.0, NEG)

    s_own_a = _dot_nt(key_rows(ia), qs_scr[0]) + causal
    s_own_b = _dot_nt(key_rows(ib), qs_scr[1]) + causal
    mx_a = fold_max(s_own_a)
    mx_b = fold_max(s_own_b)
    trips = []
    for t in range(nb - 1):
        is_a = t < p
        which = jnp.where(is_a, 0, 1)
        jb = jnp.where(is_a, t, t - p)
        trips.append((is_a, which, jb))
        s = _dot_nt(key_rows(jb), qs_scr[which])
        s_scr[t] = s
        f = fold_max(s)
        mx_a = jnp.maximum(mx_a, jnp.where(is_a, f, NEG))
        mx_b = jnp.maximum(mx_b, jnp.where(is_a, NEG, f))
    m_a = jnp.max(mx_a, axis=0, keepdims=True)
    m_b = jnp.max(mx_b, axis=0, keepdims=True)

    acc_scr[0] = _dot(vt_scr[ia], jnp.exp(s_own_a - m_a).astype(BF16))
    acc_scr[1] = _dot(vt_scr[ib], jnp.exp(s_own_b - m_b).astype(BF16))
    for t, (is_a, which, jb) in enumerate(trips):
        pr = jnp.exp(s_scr[t] - jnp.where(is_a, m_a, m_b))
        acc_scr[which] += _dot(vt_scr[jb], pr.astype(BF16))
    for which, o_ref in ((0, oa_ref), (1, ob_ref)):
        acc = acc_scr[which]
        o_ref[...] = (acc[:hd] / acc[hd:hd + 1]).T.astype(BF16)


def _moba(proj, batch, seq):
    blk = MOBA_BLOCK
    nb = seq // blk
    assert nb % 2 == 0 and nb <= HEAD_DIM
    half = nb // 2
    H = MOBA_HEADS
    hd = HEAD_DIM
    width = H * hd
    lo, hi = pl.pallas_call(
        functools.partial(_moba_kernel, nb=nb, blk=blk, scale=hd ** -0.5),
        grid=(batch, H, half),
        in_specs=[
            pl.BlockSpec((blk, hd), lambda b, h, p: (b * nb + p, h)),
            pl.BlockSpec((blk, hd), lambda b, h, p: (b * nb + nb - 1 - p, h)),
            pl.BlockSpec((seq, hd), lambda b, h, p: (b, H + h)),
            pl.BlockSpec((seq, hd), lambda b, h, p: (b, 2 * H + h)),
        ],
        out_specs=[
            pl.BlockSpec((blk, hd), lambda b, h, p: (b * half + p, h)),
            pl.BlockSpec((blk, hd), lambda b, h, p: (b * half + half - 1 - p, h)),
        ],
        out_shape=[jax.ShapeDtypeStruct((batch * half * blk, width), BF16)] * 2,
        scratch_shapes=[
            pltpu.VMEM((nb, hd), F32),
            pltpu.VMEM((seq, 2 * hd), BF16),
            pltpu.VMEM((nb, hd + MOBA_ONES_ROWS, blk), BF16),
            pltpu.VMEM((2, blk, 2 * hd), BF16),
            pltpu.VMEM((nb - 1, blk, blk), F32),
            pltpu.VMEM((2, hd + MOBA_ONES_ROWS, blk), F32),
        ],
        compiler_params=pltpu.CompilerParams(
            dimension_semantics=("parallel", "parallel", "arbitrary"), vmem_limit_bytes=VMEM_LIMIT),
        name="moba",
    )(proj, proj, proj, proj)
    shape3 = (batch, half * blk, width)
    return jnp.concatenate([lo.reshape(shape3), hi.reshape(shape3)], axis=1).reshape(batch * seq, width)


def _gla_levels(c):
    out = []
    m = c // 2
    while m >= 1:
        out.append(m)
        m //= 2
    return out


def _gla_kernel(q_ref, k_ref, v_ref, g_ref, la_ref, nw_ref, o_ref, st_scr, *,
                chunk, group, heads, scale):
    C = chunk
    levels = _gla_levels(C)

    @pl.when(pl.program_id(2) == 0)
    def _():
        st_scr[...] = jnp.zeros_like(st_scr)

    t_io = lax.broadcasted_iota(I32, (C, C), 0)
    s_io = lax.broadcasted_iota(I32, (C, C), 1)
    sel_rows = [jnp.where(s_io <= t_io, 1.0, 0.0)]
    masks = []
    for m in levels:
        blk_t = t_io // (2 * m)
        bound = blk_t * (2 * m) + (m - 1)
        sel_rows.append(jnp.where(s_io <= bound, 1.0, 0.0))
        masks.append((blk_t == s_io // (2 * m)) & (t_io % (2 * m) >= m) & (s_io % (2 * m) < m))
    cum_mat = jnp.concatenate(sel_rows, axis=0)
    diag = t_io == s_io
    nw = nw_ref[...]

    dk, dv = GLA_DK, GLA_DV
    for c in range(group):
        rows = slice(c * C, (c + 1) * C)
        for hh in range(heads):
            kc = slice(hh * dk, (hh + 1) * dk)
            vc = slice(hh * dv, (hh + 1) * dv)
            la = la_ref[rows, kc]
            cums = _dot(cum_mat, la, precision=HIGHEST)
            b = cums[:C]
            q = q_ref[rows, kc].astype(F32) * scale
            k = k_ref[rows, kc].astype(F32)
            v = v_ref[rows, vc]
            a = jnp.where(diag, _dot_nt(q.astype(BF16), k.astype(BF16)), 0.0)
            for li in range(len(levels)):
                e = jnp.exp(-jnp.abs(b - cums[(li + 1) * C:(li + 2) * C]))
                a = a + jnp.where(masks[li], _dot_nt((q * e).astype(BF16), (k * e).astype(BF16)), 0.0)
            st = st_scr[hh]
            o = _dot(a.astype(BF16), v) + _dot_nt((q * jnp.exp(b)).astype(BF16), st.astype(BF16))
            b_last = b[C - 1:C, :]
            ke = (k * jnp.exp(b_last - b)).astype(BF16)
            st_scr[hh] = st * jnp.exp(b_last) + _dot_tn(v, ke)
            gg = g_ref[rows, vc].astype(F32)
            o_ref[rows, vc] = (_rms(o, nw) * (gg * jax.nn.sigmoid(gg))).astype(BF16)


def _gla(proj, la, norm_w, batch, seq):
    T = proj.shape[0]
    Hg = GLA_HEADS
    rows = GLA_CHUNK * GLA_GROUP
    ng = seq // rows
    hs = GLA_HEADS_PER_STEP
    wk, wv = hs * GLA_DK, hs * GLA_DV
    q0 = (3 * MOBA_HEADS * HEAD_DIM) // wk
    k0 = q0 + Hg // hs
    v0 = (3 * MOBA_HEADS * HEAD_DIM + 2 * Hg * GLA_DK) // wv
    g0 = v0 + Hg // hs
    return pl.pallas_call(
        functools.partial(_gla_kernel, chunk=GLA_CHUNK, group=GLA_GROUP, heads=hs,
                          scale=GLA_DK ** -0.5),
        grid=(batch, Hg // hs, ng),
        in_specs=[
            pl.BlockSpec((rows, wk), lambda b, h, g: (b * ng + g, q0 + h)),
            pl.BlockSpec((rows, wk), lambda b, h, g: (b * ng + g, k0 + h)),
            pl.BlockSpec((rows, wv), lambda b, h, g: (b * ng + g, v0 + h)),
            pl.BlockSpec((rows, wv), lambda b, h, g: (b * ng + g, g0 + h)),
            pl.BlockSpec((rows, wk), lambda b, h, g: (b * ng + g, h)),
            pl.BlockSpec((1, GLA_DV), lambda b, h, g: (0, 0)),
        ],
        out_specs=pl.BlockSpec((rows, wv), lambda b, h, g: (b * ng + g, h)),
        out_shape=jax.ShapeDtypeStruct((T, Hg * GLA_DV), BF16),
        scratch_shapes=[pltpu.VMEM((hs, GLA_DV, GLA_DK), F32)],
        compiler_params=pltpu.CompilerParams(
            dimension_semantics=("parallel", "parallel", "arbitrary"), vmem_limit_bytes=VMEM_LIMIT),
        name="gla",
    )(proj, proj, proj, proj, la, norm_w)


def _split_bf16(x):
    hi = x.astype(BF16)
    return hi, (x - hi.astype(F32)).astype(BF16)


def _pack_bf16_pairs(x):
    n = x.shape[1] // 2
    lo = lax.bitcast_convert_type(x[:, :n].astype(BF16).astype(F32), U32)
    hi = lax.bitcast_convert_type(x[:, n:].astype(BF16).astype(F32), U32)
    return (lo >> 16) | (hi & jnp.uint32(0xFFFF0000))


def _unpack_bf16_pairs(p):
    lo = lax.bitcast_convert_type(p << 16, F32).astype(BF16)
    hi = lax.bitcast_convert_type(p & jnp.uint32(0xFFFF0000), F32).astype(BF16)
    return lo, hi


def _out_proj_kernel(mo_ref, gl_ref, x_ref, wo_ref, nw_ref, rwh_ref, rwl_ref, rb_ref,
                     x1_ref, h2p_ref, idx_ref, gate_ref):
    wm = mo_ref.shape[1]
    x1 = x_ref[...] + _dot(mo_ref[...], wo_ref[:wm, :]) + _dot(gl_ref[...], wo_ref[wm:, :])
    x1_ref[...] = x1
    h2 = _rms(x1, nw_ref[...])
    h2p_ref[...] = _pack_bf16_pairs(h2)
    h_hi, h_lo = _split_bf16(h2)
    rw_hi = rwh_ref[...]
    logits = _dot_nt(rw_hi, h_hi) + (_dot_nt(rwl_ref[...], h_hi) + _dot_nt(rw_hi, h_lo))
    tm = logits.shape[1]
    logits = logits + jnp.concatenate([rb_ref[...]] * (tm // LANES), axis=1)
    row = lax.broadcasted_iota(I32, logits.shape, 0)
    vals, idxs = [], []
    for _ in range(TOP_K):
        mx = jnp.max(logits, axis=0, keepdims=True)
        ix = jnp.min(jnp.where(logits == mx, row, LANES), axis=0, keepdims=True)
        vals.append(mx)
        idxs.append(ix)
        logits = jnp.where(row == ix, -jnp.inf, logits)
    exps = [jnp.exp(v - vals[0]) for v in vals]
    denom = exps[0]
    for e in exps[1:]:
        denom = denom + e
    idx_t = jnp.zeros(logits.shape, I32)
    gate_t = jnp.zeros(logits.shape, F32)
    for r in range(TOP_K):
        idx_t = jnp.where(row == r, idxs[r], idx_t)
        gate_t = jnp.where(row == r, exps[r] / denom, gate_t)
    idx_ref[...] = idx_t.astype(F32).T.astype(I32)
    gate_ref[...] = gate_t.T


def _out_proj(moba_o, gla_o, x2d, w_out, norm_w, rw_hi, rw_lo, rb_tile):
    T, D = x2d.shape
    tm = PROJ_TM
    wm, wg = moba_o.shape[1], gla_o.shape[1]
    return pl.pallas_call(
        _out_proj_kernel,
        grid=(T // tm,),
        in_specs=[
            pl.BlockSpec((tm, wm), lambda i: (i, 0)),
            pl.BlockSpec((tm, wg), lambda i: (i, 0)),
            pl.BlockSpec((tm, D), lambda i: (i, 0)),
            pl.BlockSpec((wm + wg, D), lambda i: (0, 0)),
            pl.BlockSpec((1, D), lambda i: (0, 0)),
            pl.BlockSpec((LANES, D), lambda i: (0, 0)),
            pl.BlockSpec((LANES, D), lambda i: (0, 0)),
            pl.BlockSpec((LANES, LANES), lambda i: (0, 0)),
        ],
        out_specs=[
            pl.BlockSpec((tm, D), lambda i: (i, 0)),
            pl.BlockSpec((tm, D // 2), lambda i: (i, 0)),
            pl.BlockSpec((tm, LANES), lambda i: (i, 0)),
            pl.BlockSpec((tm, LANES), lambda i: (i, 0)),
        ],
        out_shape=[
            jax.ShapeDtypeStruct((T, D), F32),
            jax.ShapeDtypeStruct((T, D // 2), U32),
            jax.ShapeDtypeStruct((T, LANES), I32),
            jax.ShapeDtypeStruct((T, LANES), F32),
        ],
        compiler_params=pltpu.CompilerParams(
            dimension_semantics=("parallel",), vmem_limit_bytes=VMEM_LIMIT),
        name="out_proj",
    )(moba_o, gla_o, x2d, w_out, norm_w, rw_hi, rw_lo, rb_tile)


def _moe_row_in(h_hbm, xbuf, sem, slot, tok, r):
    return pltpu.make_async_copy(h_hbm.at[pl.ds(tok, 1), :], xbuf.at[slot, pl.ds(r, 1), :],
                                 sem.at[slot])


def _moe_row_out(acc, y_hbm, sem, r, dst_row):
    return pltpu.make_async_copy(acc.at[pl.ds(r, 1), :], y_hbm.at[pl.ds(dst_row, 1), :], sem)


def _moe_blk_out(acc, y_hbm, sem, blk):
    dst = y_hbm.at[pl.ds(pl.multiple_of(blk * MOE_SUB, MOE_SUB), MOE_SUB), :]
    return pltpu.make_async_copy(acc.at[pl.ds(0, MOE_SUB), :], dst, sem)


def _moe_kernel(se_ref, rs_ref, ns_ref, asg_ref,
                h_hbm, wg_ref, wu_ref, wd_ref, bg_ref, bu_ref, bd_ref, y_hbm,
                xbuf, acc, wgb, wub, wdb, gsem, osem, *, n_tok, nf):
    s = pl.program_id(0)
    j = pl.program_id(1)
    n_grid = pl.num_programs(0)
    nsub = ns_ref[s]
    base = rs_ref[s] * MOE_SUB
    slot = s & 1
    per_trip = MOE_SUB // nf
    half = wg_ref.shape[1] // 2

    def src_token(p):
        a = asg_ref[p]
        k = (a >= n_tok).astype(I32)
        for kk in range(2, TOP_K):
            k = k + (a >= kk * n_tok).astype(I32)
        return jnp.minimum(a - k * n_tok, n_tok - 1)

    def wait_rows(n_sub_tiles):
        def wait(gidx, c):
            for u in range(MOE_DMA_UNROLL):
                _moe_row_in(h_hbm, xbuf, gsem, slot, 0, gidx * MOE_DMA_UNROLL + u).wait()
            return c

        lax.fori_loop(0, n_sub_tiles * (MOE_SUB // MOE_DMA_UNROLL), wait, 0)

    @pl.when(j == 0)
    def _():
        @pl.when(s == 0)
        def _():
            def issue(gidx, c):
                for u in range(MOE_DMA_UNROLL):
                    r = gidx * MOE_DMA_UNROLL + u
                    _moe_row_in(h_hbm, xbuf, gsem, slot, src_token(base + r), r).start()
                return c

            lax.fori_loop(0, nsub * (MOE_SUB // MOE_DMA_UNROLL), issue, 0)

        wait_rows(jnp.where(s == 0, nsub, ns_ref[jnp.maximum(s - 1, 0)]))

    @pl.when(nsub > 0)
    def _():
        @pl.when(j == 0)
        def _():
            bias_rows = jnp.broadcast_to(bd_ref[0], (MOE_SUB, acc.shape[1]))

            def init(r, c):
                acc[pl.ds(pl.multiple_of(r * MOE_SUB, MOE_SUB), MOE_SUB), :] = bias_rows
                return c

            lax.fori_loop(0, nsub, init, 0)

        bg = bg_ref[0]
        bu = bu_ref[0]
        next_base = rs_ref[s + 1] * MOE_SUB

        def tile(r, n, emit):
            rows = n * MOE_SUB
            off = pl.multiple_of(r * MOE_SUB, MOE_SUB)
            xl, xh = _unpack_bf16_pairs(xbuf[slot, pl.ds(off, rows), :])
            g = (_dot(xl, wg_ref[0, :half, :].astype(BF16))
                 + _dot(xh, wg_ref[0, half:, :].astype(BF16)) + bg)
            u = (_dot(xl, wu_ref[0, :half, :].astype(BF16))
                 + _dot(xh, wu_ref[0, half:, :].astype(BF16)) + bu)
            g = jnp.minimum(g, SWIGLU_LIMIT)
            u = jnp.clip(u, -SWIGLU_LIMIT, SWIGLU_LIMIT)
            act = (u + 1.0) * (g * jax.nn.sigmoid(SWIGLU_ALPHA * g))
            acc[pl.ds(off, rows), :] += _dot(act.astype(BF16), wd_ref[0].astype(BF16))
            n_in = n * per_trip
            g0 = pl.multiple_of((j * nsub + r) * per_trip, per_trip)
            x_next = xbuf.at[1 - slot, pl.ds(g0, n_in), :]
            for v in range(n_in):
                tok = src_token(next_base + g0 + v)
                pltpu.make_async_copy(h_hbm.at[pl.ds(tok, 1), :], x_next.at[pl.ds(v, 1), :],
                                      gsem.at[1 - slot]).start()
            if emit:
                acc_rows = acc.at[pl.ds(off, rows), :]
                for v in range(rows):
                    _moe_row_out(acc_rows, y_hbm, osem, v, asg_ref[base + off + v]).start()

        def sweep(emit):
            def pair(t, c):
                tile(2 * t, 2, emit)
                return c

            lax.fori_loop(0, nsub >> 1, pair, 0)

            @pl.when((nsub & 1) == 1)
            def _():
                tile(nsub - 1, 1, emit)

        @pl.when(j < nf - 1)
        def _():
            sweep(False)

        @pl.when(j == nf - 1)
        def _():
            sweep(True)

            def done(gidx, c):
                for u in range(MOE_DMA_UNROLL):
                    _moe_row_out(acc, y_hbm, osem, 0, 0).wait()
                return c

            lax.fori_loop(0, nsub * (MOE_SUB // MOE_DMA_UNROLL), done, 0)

    @pl.when((s == n_grid - 1) & (j == nf - 1))
    def _():
        tail0 = rs_ref[n_grid]
        n_blk = y_hbm.shape[0] // MOE_SUB
        acc[0:MOE_SUB, :] = jnp.zeros((MOE_SUB, acc.shape[1]), F32)

        def put(b, c):
            _moe_blk_out(acc, y_hbm, osem, b).start()
            return c

        lax.fori_loop(tail0, n_blk, put, 0)

        def done(b, c):
            _moe_blk_out(acc, y_hbm, osem, b).wait()
            return c

        lax.fori_loop(tail0, n_blk, done, 0)


def _moe(sup_e, sup_blk, sup_nsub, row_asg, h2p, w_gu, b_gu, w_dn, b_dn, n_rows):
    T, DP = h2p.shape
    D = 2 * DP
    E, _, F2 = w_gu.shape
    F = F2 // 2
    tf = MOE_TF
    nf = F // tf
    assert MOE_SUB % nf == 0
    n_sup = sup_e.shape[0]

    def wmap(col0):
        def f(s, j, se, rs, ns, tok):
            return (se[s], 0, col0 + jnp.where(ns[s] > 0, j, nf - 1))
        return f

    def dmap(s, j, se, rs, ns, tok):
        return (se[s], jnp.where(ns[s] > 0, j, nf - 1), 0)

    def bdmap(s, j, se, rs, ns, tok):
        return (se[s], 0, 0)

    grid_spec = pltpu.PrefetchScalarGridSpec(
        num_scalar_prefetch=4,
        grid=(n_sup, nf),
        in_specs=[
            pl.BlockSpec(memory_space=pl.ANY),
            pl.BlockSpec((1, D, tf), wmap(0)),
            pl.BlockSpec((1, D, tf), wmap(nf)),
            pl.BlockSpec((1, tf, D), dmap),
            pl.BlockSpec((1, 1, tf), wmap(0)),
            pl.BlockSpec((1, 1, tf), wmap(nf)),
            pl.BlockSpec((1, 1, D), bdmap),
        ],
        out_specs=pl.BlockSpec(memory_space=pl.ANY),
        scratch_shapes=[
            pltpu.VMEM((2, MOE_SUP, DP), U32),
            pltpu.VMEM((MOE_SUP, D), F32),
            pltpu.VMEM((D, tf), BF16),
            pltpu.VMEM((D, tf), BF16),
            pltpu.VMEM((tf, D), BF16),
            pltpu.SemaphoreType.DMA((2,)),
            pltpu.SemaphoreType.DMA,
        ],
    )
    return pl.pallas_call(
        functools.partial(_moe_kernel, n_tok=T, nf=nf),
        grid_spec=grid_spec,
        out_shape=jax.ShapeDtypeStruct((n_rows, D), F32),
        compiler_params=pltpu.CompilerParams(
            dimension_semantics=("arbitrary", "arbitrary"), vmem_limit_bytes=VMEM_LIMIT),
        name="moe",
    )(sup_e, sup_blk, sup_nsub, row_asg, h2p, w_gu, w_gu, w_dn,
      b_gu.reshape(E, 1, F2), b_gu.reshape(E, 1, F2), b_dn.reshape(E, 1, D))


def _combine_kernel(*refs):
    y_refs = refs[:TOP_K]
    x1_ref, gate_ref, nw_ref, o_ref = refs[TOP_K:]
    gates = gate_ref[...]
    x2 = x1_ref[...]
    for kk in range(TOP_K):
        x2 = x2 + gates[:, kk:kk + 1] * y_refs[kk][...]
    o_ref[...] = _rms(x2, nw_ref[...])


def _combine(y_slots, x1, gates, norm_w):
    T, D = x1.shape
    tm = COMB_TM
    tiles = T // tm

    def plane(kk):
        return pl.BlockSpec((tm, D), lambda i: (kk * tiles + i, 0))

    return pl.pallas_call(
        _combine_kernel,
        grid=(tiles,),
        in_specs=[plane(kk) for kk in range(TOP_K)] + [
            pl.BlockSpec((tm, D), lambda i: (i, 0)),
            pl.BlockSpec((tm, LANES), lambda i: (i, 0)),
            pl.BlockSpec((1, D), lambda i: (0, 0)),
        ],
        out_specs=pl.BlockSpec((tm, D), lambda i: (i, 0)),
        out_shape=jax.ShapeDtypeStruct((T, D), F32),
        compiler_params=pltpu.CompilerParams(
            dimension_semantics=("parallel",), vmem_limit_bytes=VMEM_LIMIT),
        name="combine",
    )(*([y_slots] * TOP_K), x1, gates, norm_w)


def _route(top_idx, n_experts):
    T = top_idx.shape[0]
    A = T * TOP_K
    G, SUP = MOE_SUB, MOE_SUP
    flat_e = top_idx.reshape(A)
    onehot = (flat_e[:, None] == jnp.arange(n_experts, dtype=I32)[None, :]).astype(I32)
    csum = jnp.cumsum(onehot, axis=0)
    counts = csum[-1]
    rank = jnp.sum((csum - 1) * onehot, axis=1)
    pcounts = (counts + G - 1) // G * G
    pends = jnp.cumsum(pcounts)
    pstarts = pends - pcounts
    dest = (pstarts[flat_e] + rank).astype(I32)
    n_rows = (A + n_experts * (G - 1)) // G * G
    p_ids = jnp.arange(n_rows, dtype=I32)
    seg = jnp.minimum(jnp.sum((p_ids[:, None] >= pends[None, :]).astype(I32), axis=1), n_experts - 1)
    pad_ids = A + p_ids - jnp.cumsum(counts)[seg]
    a_ids = jnp.arange(A, dtype=I32)
    slot_ids = (a_ids % TOP_K) * T + a_ids // TOP_K
    row_asg = pad_ids.astype(I32).at[dest].set(slot_ids)
    row_asg = jnp.concatenate([row_asg, jnp.zeros((SUP,), I32)])

    n_sup = (A + n_experts * (G - 1) + n_experts * (SUP - G)) // SUP + 1
    sub_e = pcounts // G
    nsup_e = (pcounts + SUP - 1) // SUP
    sup_ends = jnp.cumsum(nsup_e)
    total = sup_ends[-1]
    s_ids = jnp.arange(n_sup, dtype=I32)
    s_eff = jnp.minimum(s_ids, total - 1)
    e_of = jnp.sum((s_eff[:, None] >= sup_ends[None, :]).astype(I32), axis=1)
    local = s_eff - (sup_ends - nsup_e)[e_of]
    active = s_ids < total
    n_e = jnp.maximum(nsup_e[e_of], 1)
    small, extra = sub_e[e_of] // n_e, sub_e[e_of] % n_e
    blk0 = jnp.where(active, pstarts[e_of] // G + local * small + jnp.minimum(local, extra), 0)
    nsub = jnp.where(active, small + (local < extra).astype(I32), 0)
    order = jnp.argsort(-nsub, stable=True)
    e_sorted = e_of[order]
    sup_e = jnp.where(active, e_sorted, e_sorted[total - 1]).astype(I32)
    sup_blk = jnp.concatenate([blk0[order], pends[-1:] // G]).astype(I32)
    sup_nsub = nsub[order].astype(I32)
    return row_asg, n_rows, sup_e, sup_blk, sup_nsub


def _rope_tables(seq):
    half = HEAD_DIM // 2
    inv_freq = 1.0 / (ROPE_THETA ** (jnp.arange(0, HEAD_DIM, 2, dtype=F32) / HEAD_DIM))
    ang = jnp.arange(seq, dtype=F32)[:, None] * inv_freq[None, :]
    cos, sin = jnp.cos(ang), jnp.sin(ang)
    assert cos.shape[1] == half
    return jnp.concatenate([cos, cos], axis=1), jnp.concatenate([-sin, sin], axis=1)


def kernel(x, attn_norm_w, w_in, gla_gate_w2, gla_gate_b, gla_norm_w, w_out, ffn_norm_w,
           router_w, router_b, w_gate_up, b_gate_up, w_down, b_down, final_norm_w):
    B, S, D = x.shape
    T = B * S
    assert w_in.shape[0] == 1, "single-layer stack: the final norm is fused into combine"
    n_experts = router_w.shape[-1]
    n_main = w_in.shape[-1] - GLA_GATE_RANK
    cosf, sinf = _rope_tables(S)
    x2d = x.reshape(T, D)
    w_main = w_in[0, :, :n_main].astype(BF16)
    w_gz = jnp.pad(w_in[0, :, n_main:], ((0, 0), (0, LANES - GLA_GATE_RANK))).astype(BF16)
    w2_pad = jnp.pad(gla_gate_w2[0], ((0, LANES - GLA_GATE_RANK), (0, 0)))
    proj, log_a = _in_proj(x2d, attn_norm_w[0][None, :], w_main, w_gz, w2_pad,
                           gla_gate_b[0][None, :], cosf, sinf, S)
    moba_o = _moba(proj, B, S)
    gla_o = _gla(proj, log_a, gla_norm_w[0][None, :], B, S)
    rw_t = jnp.pad(router_w[0].T, ((0, LANES - n_experts), (0, 0)))
    rw_hi = rw_t.astype(BF16)
    rw_lo = (rw_t - rw_hi.astype(F32)).astype(BF16)
    rb_col = jnp.pad(router_b[0], (0, LANES - n_experts), constant_values=NEG)
    rb_tile = jnp.broadcast_to(rb_col[:, None], (LANES, LANES))
    x1, h2p, top_idx, gates = _out_proj(moba_o, gla_o, x2d, w_out[0].astype(BF16),
                                        ffn_norm_w[0][None, :], rw_hi, rw_lo, rb_tile)
    row_asg, n_rows, sup_e, sup_blk, sup_nsub = _route(top_idx[:, :TOP_K], n_experts)
    y_slots = _moe(sup_e, sup_blk, sup_nsub, row_asg, h2p, w_gate_up[0], b_gate_up[0],
                   w_down[0], b_down[0], n_rows)
    out = _combine(y_slots, x1, gates, final_norm_w[None, :])
    return out.reshape(B, S, D)
```

```python
import functools

import jax
import jax.numpy as jnp
from jax import lax
from jax.experimental import pallas as pl
from jax.experimental.pallas import tpu as pltpu

F32 = jnp.float32
BF16 = jnp.bfloat16
I32 = jnp.int32
U32 = jnp.uint32
HIGHEST = lax.Precision.HIGHEST

HEAD_DIM = 128
MOBA_HEADS = 8
MOBA_BLOCK = 256
MOBA_TOPK = 3
ROPE_THETA = 10000.0
GLA_HEADS = 4
GLA_DK = 128
GLA_DV = 256
GLA_GATE_RANK = 16
GLA_GATE_NORMALIZER = 16.0
TOP_K = 4
SWIGLU_ALPHA = 1.702
SWIGLU_LIMIT = 7.0
NORM_EPS = 1e-5

LANES = 128
VMEM_LIMIT = 58 * 1024 * 1024
NEG = -1e30
LOG2_E = 1.4426950408889634

IN_PROJ_TM = 1024
PROJ_TM = 512
PROJ_TN = 1024
GLA_CHUNK = 64
GLA_GROUP = 8
GLA_HEADS_PER_STEP = 2
MOE_SUB = 256
MOE_SUP = 2048
MOE_TF = 256
MOE_DMA_UNROLL = 8
COMB_TM = 256


def _dot(a, b, **kw):
    return jnp.dot(a, b, preferred_element_type=F32, **kw)


def _dot_nt(a, b, **kw):
    return lax.dot_general(a, b, (((1,), (1,)), ((), ())), preferred_element_type=F32, **kw)


def _dot_tn(a, b, **kw):
    return lax.dot_general(a, b, (((0,), (0,)), ((), ())), preferred_element_type=F32, **kw)


def _rms(x, w):
    return x * lax.rsqrt(jnp.mean(x * x, axis=-1, keepdims=True) + NORM_EPS) * w


def _in_proj_kernel(x_ref, nw_ref, w_ref, wz_ref, w2_ref, gb_ref, cos_ref, sin_ref,
                    proj_ref, la_ref, h_scr, *, n_main, n_rope):
    j = pl.program_id(1)

    @pl.when(j == 0)
    def _():
        h_scr[...] = _rms(x_ref[...], nw_ref[...]).astype(BF16)

    @pl.when(j < n_rope)
    def _():
        acc = _dot(h_scr[...], w_ref[...])
        cosf = cos_ref[...]
        sinf = sin_ref[...]
        for hd in range(acc.shape[1] // HEAD_DIM):
            t = acc[:, hd * HEAD_DIM:(hd + 1) * HEAD_DIM]
            r = t * cosf + pltpu.roll(t, HEAD_DIM // 2, axis=1) * sinf
            proj_ref[:, hd * HEAD_DIM:(hd + 1) * HEAD_DIM] = r.astype(BF16)

    @pl.when((j >= n_rope) & (j < n_main))
    def _():
        proj_ref[...] = _dot(h_scr[...], w_ref[...]).astype(BF16)

    @pl.when(j == n_main)
    def _():
        gz = _dot(h_scr[...], wz_ref[...])
        logits = _dot(gz, w2_ref[...], precision=HIGHEST) + gb_ref[...]
        logsig = jnp.minimum(logits, 0.0) - jnp.log1p(jnp.exp(-jnp.abs(logits)))
        la_ref[...] = logsig * (1.0 / GLA_GATE_NORMALIZER)


def _in_proj(x2d, norm_w, w_main, w_gz, w2_pad, gate_b, cosf, sinf, seq):
    T, D = x2d.shape
    n_cols = w_main.shape[1]
    tm, tn = IN_PROJ_TM, PROJ_TN
    n_main = n_cols // tn
    n_rope = (2 * MOBA_HEADS * HEAD_DIM) // tn
    kw = w2_pad.shape[1]
    s_tiles = seq // tm
    last = n_main - 1
    return pl.pallas_call(
        functools.partial(_in_proj_kernel, n_main=n_main, n_rope=n_rope),
        grid=(T // tm, n_main + 1),
        in_specs=[
            pl.BlockSpec((tm, D), lambda i, j: (i, 0)),
            pl.BlockSpec((1, D), lambda i, j: (0, 0)),
            pl.BlockSpec((D, tn), lambda i, j: (0, jnp.minimum(j, last))),
            pl.BlockSpec((D, LANES), lambda i, j: (0, 0)),
            pl.BlockSpec((LANES, kw), lambda i, j: (0, 0)),
            pl.BlockSpec((1, kw), lambda i, j: (0, 0)),
            pl.BlockSpec((tm, HEAD_DIM), lambda i, j: (i % s_tiles, 0)),
            pl.BlockSpec((tm, HEAD_DIM), lambda i, j: (i % s_tiles, 0)),
        ],
        out_specs=[
            pl.BlockSpec((tm, tn), lambda i, j: (i, jnp.minimum(j, last))),
            pl.BlockSpec((tm, kw), lambda i, j: (i, 0)),
        ],
        out_shape=[
            jax.ShapeDtypeStruct((T, n_cols), BF16),
            jax.ShapeDtypeStruct((T, kw), F32),
        ],
        scratch_shapes=[pltpu.VMEM((tm, D), BF16)],
        compiler_params=pltpu.CompilerParams(
            dimension_semantics=("parallel", "arbitrary"), vmem_limit_bytes=VMEM_LIMIT),
        name="in_proj",
    )(x2d, norm_w, w_main, w_gz, w2_pad, gate_b, cosf, sinf)


MOBA_ONES_ROWS = 16


def _moba_select(q_ref, kmean_scr, qs_scr, which, iq, *, nb, blk, scale):
    hd = HEAD_DIM
    qf = q_ref[...].astype(F32)
    gate = _dot_nt(kmean_scr[...], qf, precision=HIGHEST)
    n_iota = lax.broadcasted_iota(I32, (nb, blk), 0)
    rank = jnp.zeros((nb, blk), I32)
    for m in range(nb):
        gm = gate[m:m + 1, :]
        beats = (gm > gate) | ((gm == gate) & (m < n_iota))
        rank = rank + jnp.where(beats & (m < iq), 1, 0)
    keep = ((n_iota < iq) & (rank < MOBA_TOPK)) | (n_iota == iq)
    maskv = jnp.where(keep, 0.0, NEG)
    maskv = jnp.concatenate([maskv, jnp.zeros((hd - nb, blk), F32)], axis=0)
    qs_scr[which, :, :hd] = (qf * scale).astype(BF16)
    qs_scr[which, :, hd:] = maskv.T.astype(BF16)


def _moba_kernel(qa_ref, qb_ref, k_ref, v_ref, oa_ref, ob_ref,
                 kmean_scr, kaug_scr, vt_scr, qs_scr, s_scr, acc_scr, *, nb, blk, scale):
    p = pl.program_id(2)
    hd = HEAD_DIM
    sub = 8

    @pl.when(p == 0)
    def _():
        col = lax.broadcasted_iota(I32, (blk, hd), 1)
        ones = jnp.ones((MOBA_ONES_ROWS, blk), BF16)
        for n in range(nb):
            rows = slice(n * blk, (n + 1) * blk)
            kb = k_ref[rows, :]
            kmean_scr[n:n + 1, :] = jnp.sum(kb.astype(F32), axis=0, keepdims=True) * (1.0 / blk)
            kaug_scr[rows, :hd] = kb
            kaug_scr[rows, hd:] = jnp.where(col == n, 1.0, 0.0).astype(BF16)
            vt_scr[n, :hd, :] = v_ref[rows, :].astype(F32).T.astype(BF16)
            vt_scr[n, hd:, :] = ones

    ia = p
    ib = nb - 1 - p
    _moba_select(qa_ref, kmean_scr, qs_scr, 0, ia, nb=nb, blk=blk, scale=scale)
    _moba_select(qb_ref, kmean_scr, qs_scr, 1, ib, nb=nb, blk=blk, scale=scale)

    def fold_max(x):
        return jnp.max(x.reshape(blk // sub, sub, blk), axis=0)

    def key_rows(jb):
        return kaug_scr[pl.ds(pl.multiple_of(jb * blk, blk), blk), :]

    kpos = lax.broadcasted_iota(I32, (blk, blk), 0)
    qpos = lax.broadcasted_iota(I32, (blk, blk), 1)
    causal = jnp.where(kpos <= qpos, 0.0, NEG)

    s_own_a = _dot_nt(key_rows(ia), qs_scr[0]) + causal
    s_own_b = _dot_nt(key_rows(ib), qs_scr[1]) + causal
    mx_a = fold_max(s_own_a)
    mx_b = fold_max(s_own_b)
    trips = []
    for t in range(nb - 1):
        is_a = t < p
        which = jnp.where(is_a, 0, 1)
        jb = jnp.where(is_a, t, t - p)
        trips.append((is_a, which, jb))
        s = _dot_nt(key_rows(jb), qs_scr[which])
        s_scr[t] = s
        f = fold_max(s)
        mx_a = jnp.maximum(mx_a, jnp.where(is_a, f, NEG))
        mx_b = jnp.maximum(mx_b, jnp.where(is_a, NEG, f))
    m_a = jnp.max(mx_a, axis=0, keepdims=True)
    m_b = jnp.max(mx_b, axis=0, keepdims=True)

    acc_scr[0] = _dot(vt_scr[ia], jnp.exp2(s_own_a - m_a).astype(BF16))
    acc_scr[1] = _dot(vt_scr[ib], jnp.exp2(s_own_b - m_b).astype(BF16))
    for t, (is_a, which, jb) in enumerate(trips):
        pr = jnp.exp2(s_scr[t] - jnp.where(is_a, m_a, m_b))
        acc_scr[which] += _dot(vt_scr[jb], pr.astype(BF16))
    for which, o_ref in ((0, oa_ref), (1, ob_ref)):
        acc = acc_scr[which]
        o_ref[...] = (acc[:hd] / acc[hd:hd + 1]).T.astype(BF16)


def _moba(proj, batch, seq):
    blk = MOBA_BLOCK
    nb = seq // blk
    assert nb % 2 == 0 and nb <= HEAD_DIM
    half = nb // 2
    H = MOBA_HEADS
    hd = HEAD_DIM
    width = H * hd
    lo, hi = pl.pallas_call(
        functools.partial(_moba_kernel, nb=nb, blk=blk, scale=hd ** -0.5 * LOG2_E),
        grid=(batch, H, half),
        in_specs=[
            pl.BlockSpec((blk, hd), lambda b, h, p: (b * nb + p, h)),
            pl.BlockSpec((blk, hd), lambda b, h, p: (b * nb + nb - 1 - p, h)),
            pl.BlockSpec((seq, hd), lambda b, h, p: (b, H + h)),
            pl.BlockSpec((seq, hd), lambda b, h, p: (b, 2 * H + h)),
        ],
        out_specs=[
            pl.BlockSpec((blk, hd), lambda b, h, p: (b * half + p, h)),
            pl.BlockSpec((blk, hd), lambda b, h, p: (b * half + half - 1 - p, h)),
        ],
        out_shape=[jax.ShapeDtypeStruct((batch * half * blk, width), BF16)] * 2,
        scratch_shapes=[
            pltpu.VMEM((nb, hd), F32),
            pltpu.VMEM((seq, 2 * hd), BF16),
            pltpu.VMEM((nb, hd + MOBA_ONES_ROWS, blk), BF16),
            pltpu.VMEM((2, blk, 2 * hd), BF16),
            pltpu.VMEM((nb - 1, blk, blk), F32),
            pltpu.VMEM((2, hd + MOBA_ONES_ROWS, blk), F32),
        ],
        compiler_params=pltpu.CompilerParams(
            dimension_semantics=("parallel", "parallel", "arbitrary"), vmem_limit_bytes=VMEM_LIMIT),
        name="moba",
    )(proj, proj, proj, proj)
    shape3 = (batch, half * blk, width)
    return jnp.concatenate([lo.reshape(shape3), hi.reshape(shape3)], axis=1).reshape(batch * seq, width)


def _gla_levels(c):
    out = []
    m = c // 2
    while m >= 1:
        out.append(m)
        m //= 2
    return out


def _gla_kernel(q_ref, k_ref, v_ref, g_ref, la_ref, nw_ref, o_ref, st_scr, *,
                chunk, group, heads, scale):
    C = chunk
    levels = _gla_levels(C)

    @pl.when(pl.program_id(2) == 0)
    def _():
        st_scr[...] = jnp.zeros_like(st_scr)

    t_io = lax.broadcasted_iota(I32, (C, C), 0)
    s_io = lax.broadcasted_iota(I32, (C, C), 1)
    sel_rows = [jnp.where(s_io <= t_io, 1.0, 0.0)]
    masks = []
    for m in levels:
        blk_t = t_io // (2 * m)
        bound = blk_t * (2 * m) + (m - 1)
        sel_rows.append(jnp.where(s_io <= bound, 1.0, 0.0))
        masks.append((blk_t == s_io // (2 * m)) & (t_io % (2 * m) >= m) & (s_io % (2 * m) < m))
    cum_mat = jnp.concatenate(sel_rows, axis=0).astype(BF16)
    diag = t_io == s_io
    nw = nw_ref[...]

    dk, dv = GLA_DK, GLA_DV
    for c in range(group):
        rows = slice(c * C, (c + 1) * C)
        for hh in range(heads):
            kc = slice(hh * dk, (hh + 1) * dk)
            vc = slice(hh * dv, (hh + 1) * dv)
            la_hi, la_lo = _split_bf16(la_ref[rows, kc])
            cums = _dot(cum_mat, la_hi) + _dot(cum_mat, la_lo)
            b = cums[:C]
            q = q_ref[rows, kc].astype(F32) * scale
            k = k_ref[rows, kc].astype(F32)
            v = v_ref[rows, vc]
            a = jnp.where(diag, _dot_nt(q.astype(BF16), k.astype(BF16)), 0.0)
            for li in range(len(levels)):
                e = jnp.exp(-jnp.abs(b - cums[(li + 1) * C:(li + 2) * C]))
                a = a + jnp.where(masks[li], _dot_nt((q * e).astype(BF16), (k * e).astype(BF16)), 0.0)
            st = st_scr[hh]
            o = _dot(a.astype(BF16), v) + _dot_nt((q * jnp.exp(b)).astype(BF16), st.astype(BF16))
            b_last = b[C - 1:C, :]
            ke = (k * jnp.exp(b_last - b)).astype(BF16)
            st_scr[hh] = st * jnp.exp(b_last) + _dot_tn(v, ke)
            gg = g_ref[rows, vc].astype(F32)
            o_ref[rows, vc] = (_rms(o, nw) * (gg * jax.nn.sigmoid(gg))).astype(BF16)


def _gla(proj, la, norm_w, batch, seq):
    T = proj.shape[0]
    Hg = GLA_HEADS
    rows = GLA_CHUNK * GLA_GROUP
    ng = seq // rows
    hs = GLA_HEADS_PER_STEP
    wk, wv = hs * GLA_DK, hs * GLA_DV
    q0 = (3 * MOBA_HEADS * HEAD_DIM) // wk
    k0 = q0 + Hg // hs
    v0 = (3 * MOBA_HEADS * HEAD_DIM + 2 * Hg * GLA_DK) // wv
    g0 = v0 + Hg // hs
    return pl.pallas_call(
        functools.partial(_gla_kernel, chunk=GLA_CHUNK, group=GLA_GROUP, heads=hs,
                          scale=GLA_DK ** -0.5),
        grid=(batch, Hg // hs, ng),
        in_specs=[
            pl.BlockSpec((rows, wk), lambda b, h, g: (b * ng + g, q0 + h)),
            pl.BlockSpec((rows, wk), lambda b, h, g: (b * ng + g, k0 + h)),
            pl.BlockSpec((rows, wv), lambda b, h, g: (b * ng + g, v0 + h)),
            pl.BlockSpec((rows, wv), lambda b, h, g: (b * ng + g, g0 + h)),
            pl.BlockSpec((rows, wk), lambda b, h, g: (b * ng + g, h)),
            pl.BlockSpec((1, GLA_DV), lambda b, h, g: (0, 0)),
        ],
        out_specs=pl.BlockSpec((rows, wv), lambda b, h, g: (b * ng + g, h)),
        out_shape=jax.ShapeDtypeStruct((T, Hg * GLA_DV), BF16),
        scratch_shapes=[pltpu.VMEM((hs, GLA_DV, GLA_DK), F32)],
        compiler_params=pltpu.CompilerParams(
            dimension_semantics=("parallel", "parallel", "arbitrary"), vmem_limit_bytes=VMEM_LIMIT),
        name="gla",
    )(proj, proj, proj, proj, la, norm_w)


def _split_bf16(x):
    hi = x.astype(BF16)
    return hi, (x - hi.astype(F32)).astype(BF16)


def _pack_bf16_pairs(x):
    n = x.shape[1] // 2
    lo = lax.bitcast_convert_type(x[:, :n].astype(BF16).astype(F32), U32)
    hi = lax.bitcast_convert_type(x[:, n:].astype(BF16).astype(F32), U32)
    return (lo >> 16) | (hi & jnp.uint32(0xFFFF0000))


def _unpack_bf16_pairs(p):
    lo = lax.bitcast_convert_type(p << 16, F32).astype(BF16)
    hi = lax.bitcast_convert_type(p & jnp.uint32(0xFFFF0000), F32).astype(BF16)
    return lo, hi


ROW_TILE = 8


def _token_tile(ref, t):
    return ref.at[pl.ds(pl.multiple_of(t * ROW_TILE, ROW_TILE), ROW_TILE), :]


def _store_token_rows(ref, x):
    n = x.shape[0]
    for c in range(ROW_TILE):
        ref[pl.ds(c, n, stride=ROW_TILE), :] = x[:, c * LANES:(c + 1) * LANES]


def _load_token_rows(ref, t0, n):
    base = pl.multiple_of(t0 * ROW_TILE, ROW_TILE)
    return jnp.concatenate(
        [ref[pl.ds(base + c, n, stride=ROW_TILE), :] for c in range(ROW_TILE)], axis=1)


def _out_proj_kernel(mo_ref, gl_ref, x_ref, wo_ref, nw_ref, rwh_ref, rwl_ref, rb_ref,
                     x1_ref, h2p_ref, idx_ref, gate_ref):
    wm = mo_ref.shape[1]
    x1 = x_ref[...] + _dot(mo_ref[...], wo_ref[:wm, :]) + _dot(gl_ref[...], wo_ref[wm:, :])
    x1_ref[...] = x1
    h2 = _rms(x1, nw_ref[...])
    _store_token_rows(h2p_ref, _pack_bf16_pairs(h2))
    h_hi, h_lo = _split_bf16(h2)
    rw_hi = rwh_ref[...]
    logits = _dot_nt(rw_hi, h_hi) + (_dot_nt(rwl_ref[...], h_hi) + _dot_nt(rw_hi, h_lo))
    tm = logits.shape[1]
    logits = logits + jnp.concatenate([rb_ref[...]] * (tm // LANES), axis=1)
    row = lax.broadcasted_iota(I32, logits.shape, 0)
    vals, idxs = [], []
    for _ in range(TOP_K):
        mx = jnp.max(logits, axis=0, keepdims=True)
        ix = jnp.min(jnp.where(logits == mx, row, LANES), axis=0, keepdims=True)
        vals.append(mx)
        idxs.append(ix)
        logits = jnp.where(row == ix, -jnp.inf, logits)
    exps = [jnp.exp(v - vals[0]) for v in vals]
    denom = exps[0]
    for e in exps[1:]:
        denom = denom + e
    idx_t = jnp.zeros(logits.shape, I32)
    gate_t = jnp.zeros(logits.shape, F32)
    for r in range(TOP_K):
        idx_t = jnp.where(row == r, idxs[r], idx_t)
        gate_t = jnp.where(row == r, exps[r] / denom, gate_t)
    idx_ref[...] = idx_t.astype(F32).T.astype(I32)
    gate_ref[...] = gate_t.T


def _out_proj(moba_o, gla_o, x2d, w_out, norm_w, rw_hi, rw_lo, rb_tile):
    T, D = x2d.shape
    tm = PROJ_TM
    wm, wg = moba_o.shape[1], gla_o.shape[1]
    return pl.pallas_call(
        _out_proj_kernel,
        grid=(T // tm,),
        in_specs=[
            pl.BlockSpec((tm, wm), lambda i: (i, 0)),
            pl.BlockSpec((tm, wg), lambda i: (i, 0)),
            pl.BlockSpec((tm, D), lambda i: (i, 0)),
            pl.BlockSpec((wm + wg, D), lambda i: (0, 0)),
            pl.BlockSpec((1, D), lambda i: (0, 0)),
            pl.BlockSpec((LANES, D), lambda i: (0, 0)),
            pl.BlockSpec((LANES, D), lambda i: (0, 0)),
            pl.BlockSpec((LANES, LANES), lambda i: (0, 0)),
        ],
        out_specs=[
            pl.BlockSpec((tm, D), lambda i: (i, 0)),
            pl.BlockSpec((tm * ROW_TILE, LANES), lambda i: (i, 0)),
            pl.BlockSpec((tm, LANES), lambda i: (i, 0)),
            pl.BlockSpec((tm, LANES), lambda i: (i, 0)),
        ],
        out_shape=[
            jax.ShapeDtypeStruct((T, D), F32),
            jax.ShapeDtypeStruct((T * ROW_TILE, LANES), U32),
            jax.ShapeDtypeStruct((T, LANES), I32),
            jax.ShapeDtypeStruct((T, LANES), F32),
        ],
        compiler_params=pltpu.CompilerParams(
            dimension_semantics=("parallel",), vmem_limit_bytes=VMEM_LIMIT),
        name="out_proj",
    )(moba_o, gla_o, x2d, w_out, norm_w, rw_hi, rw_lo, rb_tile)


def _slot_bits(n_rows):
    return (n_rows - 1).bit_length()


def _moe_row_in(h_hbm, xbuf, sem, slot, tok, r):
    return pltpu.make_async_copy(_token_tile(h_hbm, tok), _token_tile(xbuf.at[slot], r), sem.at[slot])


def _moe_row_out(acc, y_hbm, sem, r, dst_row):
    return pltpu.make_async_copy(acc.at[pl.ds(r, 1), :], y_hbm.at[pl.ds(dst_row, 1), :], sem)


def _moe_blk_out(acc, y_hbm, sem, blk):
    dst = y_hbm.at[pl.ds(pl.multiple_of(blk * MOE_SUB, MOE_SUB), MOE_SUB), :]
    return pltpu.make_async_copy(acc.at[pl.ds(0, MOE_SUB), :], dst, sem)


def _moe_kernel(se_ref, rs_ref, ns_ref, asg_ref,
                h_hbm, wg_ref, wu_ref, wd_ref, bg_ref, bu_ref, bd_ref, y_hbm,
                xbuf, acc, gsem, osem, *, slot_bits, nf):
    s = pl.program_id(0)
    j = pl.program_id(1)
    n_grid = pl.num_programs(0)
    nsub = ns_ref[s]
    base = rs_ref[s] * MOE_SUB
    slot = s & 1
    per_trip = MOE_SUB // nf
    half = wg_ref.shape[1] // 2

    def src_token(p):
        return lax.shift_right_logical(asg_ref[p], slot_bits)

    def out_slot(p):
        return asg_ref[p] & ((1 << slot_bits) - 1)

    def wait_rows(n_sub_tiles):
        def wait(gidx, c):
            for u in range(MOE_DMA_UNROLL):
                _moe_row_in(h_hbm, xbuf, gsem, slot, 0, gidx * MOE_DMA_UNROLL + u).wait()
            return c

        lax.fori_loop(0, n_sub_tiles * (MOE_SUB // MOE_DMA_UNROLL), wait, 0)

    @pl.when(j == 0)
    def _():
        @pl.when(s == 0)
        def _():
            def issue(gidx, c):
                for u in range(MOE_DMA_UNROLL):
                    r = gidx * MOE_DMA_UNROLL + u
                    _moe_row_in(h_hbm, xbuf, gsem, slot, src_token(base + r), r).start()
                return c

            lax.fori_loop(0, nsub * (MOE_SUB // MOE_DMA_UNROLL), issue, 0)

        wait_rows(jnp.where(s == 0, nsub, ns_ref[jnp.maximum(s - 1, 0)]))

    @pl.when(nsub > 0)
    def _():
        @pl.when(j == 0)
        def _():
            bias_rows = jnp.broadcast_to(bd_ref[0], (MOE_SUB, acc.shape[1]))

            def init(r, c):
                acc[pl.ds(pl.multiple_of(r * MOE_SUB, MOE_SUB), MOE_SUB), :] = bias_rows
                return c

            lax.fori_loop(0, nsub, init, 0)

        bg = bg_ref[0]
        bu = bu_ref[0]
        next_base = rs_ref[s + 1] * MOE_SUB

        def tile(r, n, emit):
            rows = n * MOE_SUB
            off = pl.multiple_of(r * MOE_SUB, MOE_SUB)
            xl, xh = _unpack_bf16_pairs(_load_token_rows(xbuf.at[slot], off, rows))
            g = (_dot(xl, wg_ref[0, :half, :].astype(BF16))
                 + _dot(xh, wg_ref[0, half:, :].astype(BF16)) + bg)
            u = (_dot(xl, wu_ref[0, :half, :].astype(BF16))
                 + _dot(xh, wu_ref[0, half:, :].astype(BF16)) + bu)
            g = jnp.minimum(g, SWIGLU_LIMIT)
            u = jnp.clip(u, -SWIGLU_LIMIT, SWIGLU_LIMIT)
            act = (u + 1.0) * (g * jax.nn.sigmoid(SWIGLU_ALPHA * g))
            acc[pl.ds(off, rows), :] += _dot(act.astype(BF16), wd_ref[0].astype(BF16))
            n_in = n * per_trip
            g0 = pl.multiple_of((j * nsub + r) * per_trip, per_trip)
            x_next = xbuf.at[1 - slot, pl.ds(g0 * ROW_TILE, n_in * ROW_TILE), :]
            for v in range(n_in):
                tok = src_token(next_base + g0 + v)
                pltpu.make_async_copy(_token_tile(h_hbm, tok), _token_tile(x_next, v),
                                      gsem.at[1 - slot]).start()
            if emit:
                acc_rows = acc.at[pl.ds(off, rows), :]
                for v in range(rows):
                    _moe_row_out(acc_rows, y_hbm, osem, v, out_slot(base + off + v)).start()

        def sweep(emit):
            def pair(t, c):
                tile(2 * t, 2, emit)
                return c

            lax.fori_loop(0, nsub >> 1, pair, 0)

            @pl.when((nsub & 1) == 1)
            def _():
                tile(nsub - 1, 1, emit)

        @pl.when(j < nf - 1)
        def _():
            sweep(False)

        @pl.when(j == nf - 1)
        def _():
            sweep(True)

            def done(gidx, c):
                for u in range(MOE_DMA_UNROLL):
                    _moe_row_out(acc, y_hbm, osem, 0, 0).wait()
                return c

            lax.fori_loop(0, nsub * (MOE_SUB // MOE_DMA_UNROLL), done, 0)

    @pl.when((s == n_grid - 1) & (j == nf - 1))
    def _():
        tail0 = rs_ref[n_grid]
        n_blk = y_hbm.shape[0] // MOE_SUB
        acc[0:MOE_SUB, :] = jnp.zeros((MOE_SUB, acc.shape[1]), F32)

        def put(b, c):
            _moe_blk_out(acc, y_hbm, osem, b).start()
            return c

        lax.fori_loop(tail0, n_blk, put, 0)

        def done(b, c):
            _moe_blk_out(acc, y_hbm, osem, b).wait()
            return c

        lax.fori_loop(tail0, n_blk, done, 0)


def _moe(sup_e, sup_blk, sup_nsub, row_asg, h2p, w_gu, b_gu, w_dn, b_dn, n_rows):
    T = h2p.shape[0] // ROW_TILE
    D = 2 * ROW_TILE * LANES
    assert w_gu.shape[1] == D
    E, _, F2 = w_gu.shape
    F = F2 // 2
    tf = MOE_TF
    nf = F // tf
    assert MOE_SUB % nf == 0
    n_sup = sup_e.shape[0]

    def wmap(col0):
        def f(s, j, se, rs, ns, tok):
            return (se[s], 0, col0 + jnp.where(ns[s] > 0, j, nf - 1))
        return f

    def dmap(s, j, se, rs, ns, tok):
        return (se[s], jnp.where(ns[s] > 0, j, nf - 1), 0)

    def bdmap(s, j, se, rs, ns, tok):
        return (se[s], 0, 0)

    grid_spec = pltpu.PrefetchScalarGridSpec(
        num_scalar_prefetch=4,
        grid=(n_sup, nf),
        in_specs=[
            pl.BlockSpec(memory_space=pl.ANY),
            pl.BlockSpec((1, D, tf), wmap(0)),
            pl.BlockSpec((1, D, tf), wmap(nf)),
            pl.BlockSpec((1, tf, D), dmap),
            pl.BlockSpec((1, 1, tf), wmap(0)),
            pl.BlockSpec((1, 1, tf), wmap(nf)),
            pl.BlockSpec((1, 1, D), bdmap),
        ],
        out_specs=pl.BlockSpec(memory_space=pl.ANY),
        scratch_shapes=[
            pltpu.VMEM((2, MOE_SUP * ROW_TILE, LANES), U32),
            pltpu.VMEM((MOE_SUP, D), F32),
            pltpu.SemaphoreType.DMA((2,)),
            pltpu.SemaphoreType.DMA,
        ],
    )
    return pl.pallas_call(
        functools.partial(_moe_kernel, slot_bits=_slot_bits(n_rows), nf=nf),
        grid_spec=grid_spec,
        out_shape=jax.ShapeDtypeStruct((n_rows, D), F32),
        compiler_params=pltpu.CompilerParams(
            dimension_semantics=("arbitrary", "arbitrary"), vmem_limit_bytes=VMEM_LIMIT),
        name="moe",
    )(sup_e, sup_blk, sup_nsub, row_asg, h2p, w_gu, w_gu, w_dn,
      b_gu.reshape(E, 1, F2), b_gu.reshape(E, 1, F2), b_dn.reshape(E, 1, D))


def _combine_kernel(*refs):
    y_refs = refs[:TOP_K]
    x1_ref, gate_ref, nw_ref, o_ref = refs[TOP_K:]
    gates = gate_ref[...]
    x2 = x1_ref[...]
    for kk in range(TOP_K):
        x2 = x2 + gates[:, kk:kk + 1] * y_refs[kk][...]
    o_ref[...] = _rms(x2, nw_ref[...])


def _combine(y_slots, x1, gates, norm_w):
    T, D = x1.shape
    tm = COMB_TM
    tiles = T // tm

    def plane(kk):
        return pl.BlockSpec((tm, D), lambda i: (kk * tiles + i, 0))

    return pl.pallas_call(
        _combine_kernel,
        grid=(tiles,),
        in_specs=[plane(kk) for kk in range(TOP_K)] + [
            pl.BlockSpec((tm, D), lambda i: (i, 0)),
            pl.BlockSpec((tm, LANES), lambda i: (i, 0)),
            pl.BlockSpec((1, D), lambda i: (0, 0)),
        ],
        out_specs=pl.BlockSpec((tm, D), lambda i: (i, 0)),
        out_shape=jax.ShapeDtypeStruct((T, D), F32),
        compiler_params=pltpu.CompilerParams(
            dimension_semantics=("parallel",), vmem_limit_bytes=VMEM_LIMIT),
        name="combine",
    )(*([y_slots] * TOP_K), x1, gates, norm_w)


def _route(top_idx, n_experts):
    T = top_idx.shape[0]
    A = T * TOP_K
    G, SUP = MOE_SUB, MOE_SUP
    flat_e = top_idx.reshape(A)
    onehot = (flat_e[:, None] == jnp.arange(n_experts, dtype=I32)[None, :]).astype(I32)
    csum = jnp.cumsum(onehot, axis=0)
    counts = csum[-1]
    rank = jnp.sum((csum - 1) * onehot, axis=1)
    pcounts = (counts + G - 1) // G * G
    pends = jnp.cumsum(pcounts)
    pstarts = pends - pcounts
    dest = (pstarts[flat_e] + rank).astype(I32)
    n_rows = (A + n_experts * (G - 1)) // G * G
    p_ids = jnp.arange(n_rows, dtype=I32)
    seg = jnp.minimum(jnp.sum((p_ids[:, None] >= pends[None, :]).astype(I32), axis=1), n_experts - 1)
    pad_ids = A + p_ids - jnp.cumsum(counts)[seg]
    a_ids = jnp.arange(A, dtype=I32)
    slot_ids = (a_ids % TOP_K) * T + a_ids // TOP_K
    bits = _slot_bits(n_rows)
    assert (T - 1) << bits < 2 ** 31
    row_asg = ((T - 1) << bits | pad_ids).astype(I32).at[dest].set((a_ids // TOP_K) << bits | slot_ids)
    row_asg = jnp.concatenate([row_asg, jnp.zeros((SUP,), I32)])

    n_sup = (A + n_experts * (G - 1) + n_experts * (SUP - G)) // SUP + 1
    sub_e = pcounts // G
    nsup_e = (pcounts + SUP - 1) // SUP
    sup_ends = jnp.cumsum(nsup_e)
    total = sup_ends[-1]
    s_ids = jnp.arange(n_sup, dtype=I32)
    s_eff = jnp.minimum(s_ids, total - 1)
    e_of = jnp.sum((s_eff[:, None] >= sup_ends[None, :]).astype(I32), axis=1)
    local = s_eff - (sup_ends - nsup_e)[e_of]
    active = s_ids < total
    n_e = jnp.maximum(nsup_e[e_of], 1)
    small, extra = sub_e[e_of] // n_e, sub_e[e_of] % n_e
    blk0 = jnp.where(active, pstarts[e_of] // G + local * small + jnp.minimum(local, extra), 0)
    nsub = jnp.where(active, small + (local < extra).astype(I32), 0)
    order = jnp.argsort(-nsub, stable=True)
    e_sorted = e_of[order]
    sup_e = jnp.where(active, e_sorted, e_sorted[total - 1]).astype(I32)
    sup_blk = jnp.concatenate([blk0[order], pends[-1:] // G]).astype(I32)
    sup_nsub = nsub[order].astype(I32)
    return row_asg, n_rows, sup_e, sup_blk, sup_nsub


def _rope_tables(seq):
    half = HEAD_DIM // 2
    inv_freq = 1.0 / (ROPE_THETA ** (jnp.arange(0, HEAD_DIM, 2, dtype=F32) / HEAD_DIM))
    ang = jnp.arange(seq, dtype=F32)[:, None] * inv_freq[None, :]
    cos, sin = jnp.cos(ang), jnp.sin(ang)
    assert cos.shape[1] == half
    return jnp.concatenate([cos, cos], axis=1), jnp.concatenate([-sin, sin], axis=1)


def kernel(x, attn_norm_w, w_in, gla_gate_w2, gla_gate_b, gla_norm_w, w_out, ffn_norm_w,
           router_w, router_b, w_gate_up, b_gate_up, w_down, b_down, final_norm_w):
    B, S, D = x.shape
    T = B * S
    assert w_in.shape[0] == 1, "single-layer stack: the final norm is fused into combine"
    n_experts = router_w.shape[-1]
    n_main = w_in.shape[-1] - GLA_GATE_RANK
    cosf, sinf = _rope_tables(S)
    x2d = x.reshape(T, D)
    w_main = w_in[0, :, :n_main].astype(BF16)
    w_gz = jnp.pad(w_in[0, :, n_main:], ((0, 0), (0, LANES - GLA_GATE_RANK))).astype(BF16)
    w2_pad = jnp.pad(gla_gate_w2[0], ((0, LANES - GLA_GATE_RANK), (0, 0)))
    proj, log_a = _in_proj(x2d, attn_norm_w[0][None, :], w_main, w_gz, w2_pad,
                           gla_gate_b[0][None, :], cosf, sinf, S)
    moba_o = _moba(proj, B, S)
    gla_o = _gla(proj, log_a, gla_norm_w[0][None, :], B, S)
    rw_t = jnp.pad(router_w[0].T, ((0, LANES - n_experts), (0, 0)))
    rw_hi = rw_t.astype(BF16)
    rw_lo = (rw_t - rw_hi.astype(F32)).astype(BF16)
    rb_col = jnp.pad(router_b[0], (0, LANES - n_experts), constant_values=NEG)
    rb_tile = jnp.broadcast_to(rb_col[:, None], (LANES, LANES))
    x1, h2p, top_idx, gates = _out_proj(moba_o, gla_o, x2d, w_out[0].astype(BF16),
                                        ffn_norm_w[0][None, :], rw_hi, rw_lo, rb_tile)
    row_asg, n_rows, sup_e, sup_blk, sup_nsub = _route(top_idx[:, :TOP_K], n_experts)
    y_slots = _moe(sup_e, sup_blk, sup_nsub, row_asg, h2p, w_gate_up[0], b_gate_up[0],
                   w_down[0], b_down[0], n_rows)
    out = _combine(y_slots, x1, gates, final_norm_w[None, :])
    return out.reshape(B, S, D)
```

```python
import functools

import jax
import jax.numpy as jnp
from jax import lax
from jax.experimental import pallas as pl
from jax.experimental.pallas import tpu as pltpu

F32 = jnp.float32
BF16 = jnp.bfloat16
I32 = jnp.int32
U32 = jnp.uint32

HEAD_DIM = 128
MOBA_HEADS = 8
MOBA_BLOCK = 256
MOBA_TOPK = 3
ROPE_THETA = 10000.0
GLA_HEADS = 4
GLA_DK = 128
GLA_DV = 256
GLA_GATE_RANK = 16
GLA_GATE_NORMALIZER = 16.0
TOP_K = 4
SWIGLU_ALPHA = 1.702
SWIGLU_LIMIT = 7.0
NORM_EPS = 1e-5

LANES = 128
VMEM_LIMIT = 58 * 1024 * 1024
NEG = -1e30
LOG2_E = 1.4426950408889634

IN_PROJ_TM = 1024
PROJ_TM = 512
PROJ_TN = 1024
GLA_CHUNK = 64
GLA_GROUP = 16
GLA_HEADS_PER_STEP = 2
MOE_SUB = 256
MOE_SUP = 2048
MOE_TF = 256
MOE_DMA_UNROLL = 8
COMB_TM = 256


def _dot(a, b, **kw):
    return jnp.dot(a, b, preferred_element_type=F32, **kw)


def _dot_nt(a, b, **kw):
    return lax.dot_general(a, b, (((1,), (1,)), ((), ())), preferred_element_type=F32, **kw)


def _dot_tn(a, b, **kw):
    return lax.dot_general(a, b, (((0,), (0,)), ((), ())), preferred_element_type=F32, **kw)


def _rms(x, w):
    return x * lax.rsqrt(jnp.mean(x * x, axis=-1, keepdims=True) + NORM_EPS) * w


def _in_proj_kernel(x_ref, nw_ref, w_ref, wz_ref, w2_ref, gb_ref, cos_ref, sin_ref,
                    proj_ref, la_ref, h_scr, *, n_main, n_rope):
    j = pl.program_id(1)

    @pl.when(j == 0)
    def _():
        h_scr[...] = _rms(x_ref[...], nw_ref[...]).astype(BF16)

    @pl.when(j < n_rope)
    def _():
        acc = _dot(h_scr[...], w_ref[...])
        cosf = cos_ref[...]
        sinf = sin_ref[...]
        for hd in range(acc.shape[1] // HEAD_DIM):
            t = acc[:, hd * HEAD_DIM:(hd + 1) * HEAD_DIM]
            r = t * cosf + pltpu.roll(t, HEAD_DIM // 2, axis=1) * sinf
            proj_ref[:, hd * HEAD_DIM:(hd + 1) * HEAD_DIM] = r.astype(BF16)

    @pl.when((j >= n_rope) & (j < n_main))
    def _():
        proj_ref[...] = _dot(h_scr[...], w_ref[...]).astype(BF16)

    @pl.when(j == n_main)
    def _():
        gz = _dot(h_scr[...], wz_ref[...])
        gz_hi, gz_lo = _split_bf16(gz)
        w2_hi, w2_lo = _split_bf16(w2_ref[...])
        logits = _dot(gz_hi, w2_hi) + (_dot(gz_hi, w2_lo) + _dot(gz_lo, w2_hi)) + gb_ref[...]
        logsig = jnp.minimum(logits, 0.0) - jnp.log1p(jnp.exp(-jnp.abs(logits)))
        la_ref[...] = logsig * (1.0 / GLA_GATE_NORMALIZER)


def _in_proj(x2d, norm_w, w_main, w_gz, w2_pad, gate_b, cosf, sinf, seq):
    T, D = x2d.shape
    n_cols = w_main.shape[1]
    tm, tn = IN_PROJ_TM, PROJ_TN
    n_main = n_cols // tn
    n_rope = (2 * MOBA_HEADS * HEAD_DIM) // tn
    kw = w2_pad.shape[1]
    s_tiles = seq // tm
    last = n_main - 1
    return pl.pallas_call(
        functools.partial(_in_proj_kernel, n_main=n_main, n_rope=n_rope),
        grid=(T // tm, n_main + 1),
        in_specs=[
            pl.BlockSpec((tm, D), lambda i, j: (i, 0)),
            pl.BlockSpec((1, D), lambda i, j: (0, 0)),
            pl.BlockSpec((D, tn), lambda i, j: (0, jnp.minimum(j, last))),
            pl.BlockSpec((D, LANES), lambda i, j: (0, 0)),
            pl.BlockSpec((LANES, kw), lambda i, j: (0, 0)),
            pl.BlockSpec((1, kw), lambda i, j: (0, 0)),
            pl.BlockSpec((tm, HEAD_DIM), lambda i, j: (i % s_tiles, 0)),
            pl.BlockSpec((tm, HEAD_DIM), lambda i, j: (i % s_tiles, 0)),
        ],
        out_specs=[
            pl.BlockSpec((tm, tn), lambda i, j: (i, jnp.minimum(j, last))),
            pl.BlockSpec((tm, kw), lambda i, j: (i, 0)),
        ],
        out_shape=[
            jax.ShapeDtypeStruct((T, n_cols), BF16),
            jax.ShapeDtypeStruct((T, kw), F32),
        ],
        scratch_shapes=[pltpu.VMEM((tm, D), BF16)],
        compiler_params=pltpu.CompilerParams(
            dimension_semantics=("parallel", "arbitrary"), vmem_limit_bytes=VMEM_LIMIT),
        name="in_proj",
    )(x2d, norm_w, w_main, w_gz, w2_pad, gate_b, cosf, sinf)


MOBA_ONES_ROWS = 16


def _moba_select(q_ref, kmean_scr, qs_scr, which, iq, *, nb, blk, scale):
    hd = HEAD_DIM
    qt = q_ref[...].astype(F32).T
    qt_b = qt.astype(BF16)
    km = kmean_scr[...]
    km_hi, km_r = km.astype(BF16), km - km.astype(BF16).astype(F32)
    km_mid, km_lo = _split_bf16(km_r)
    gate = _dot(km_hi, qt_b) + (_dot(km_mid, qt_b) + _dot(km_lo, qt_b))
    n_iota = lax.broadcasted_iota(I32, (nb, blk), 0)
    rank = jnp.zeros((nb, blk), I32)
    for m in range(nb):
        gm = gate[m:m + 1, :]
        beats = (gm > gate) | ((gm == gate) & (m < n_iota))
        rank = rank + jnp.where(beats & (m < iq), 1, 0)
    keep = ((n_iota < iq) & (rank < MOBA_TOPK)) | (n_iota == iq)
    maskv = jnp.where(keep, 0.0, NEG)
    maskv = jnp.concatenate([maskv, jnp.zeros((hd - nb, blk), F32)], axis=0)
    qs_scr[which, :hd, :] = (qt * scale).astype(BF16)
    qs_scr[which, hd:, :] = maskv.astype(BF16)


def _moba_kernel(qa_ref, qb_ref, k_ref, v_ref, oa_ref, ob_ref,
                 kmean_scr, kaug_scr, vt_scr, qs_scr, s_scr, acc_scr, *, nb, blk, scale):
    p = pl.program_id(2)
    hd = HEAD_DIM
    sub = 8

    @pl.when(p == 0)
    def _():
        col = lax.broadcasted_iota(I32, (blk, hd), 1)
        ones = jnp.ones((MOBA_ONES_ROWS, blk), BF16)
        for n in range(nb):
            rows = slice(n * blk, (n + 1) * blk)
            kb = k_ref[rows, :]
            kmean_scr[n:n + 1, :] = jnp.sum(kb.astype(F32), axis=0, keepdims=True) * (1.0 / blk)
            kaug_scr[rows, :hd] = kb
            kaug_scr[rows, hd:] = jnp.where(col == n, 1.0, 0.0).astype(BF16)
            vt_scr[n, :hd, :] = v_ref[rows, :].astype(F32).T.astype(BF16)
            vt_scr[n, hd:, :] = ones

    ia = p
    ib = nb - 1 - p
    _moba_select(qa_ref, kmean_scr, qs_scr, 0, ia, nb=nb, blk=blk, scale=scale)
    _moba_select(qb_ref, kmean_scr, qs_scr, 1, ib, nb=nb, blk=blk, scale=scale)

    def fold_max(x):
        return jnp.max(x.reshape(blk // sub, sub, blk), axis=0)

    def key_rows(jb):
        return kaug_scr[pl.ds(pl.multiple_of(jb * blk, blk), blk), :]

    kpos = lax.broadcasted_iota(I32, (blk, blk), 0)
    qpos = lax.broadcasted_iota(I32, (blk, blk), 1)
    causal = jnp.where(kpos <= qpos, 0.0, NEG)

    s_own_a = _dot(key_rows(ia), qs_scr[0]) + causal
    s_own_b = _dot(key_rows(ib), qs_scr[1]) + causal
    mx_a = fold_max(s_own_a)
    mx_b = fold_max(s_own_b)
    trips = []
    for t in range(nb - 1):
        is_a = t < p
        which = jnp.where(is_a, 0, 1)
        jb = jnp.where(is_a, t, t - p)
        trips.append((is_a, which, jb))
        s = _dot(key_rows(jb), qs_scr[which])
        s_scr[t] = s
        f = fold_max(s)
        mx_a = jnp.maximum(mx_a, jnp.where(is_a, f, NEG))
        mx_b = jnp.maximum(mx_b, jnp.where(is_a, NEG, f))
    m_a = jnp.max(mx_a, axis=0, keepdims=True)
    m_b = jnp.max(mx_b, axis=0, keepdims=True)

    acc_scr[0] = _dot(vt_scr[ia], jnp.exp2(s_own_a - m_a).astype(BF16))
    acc_scr[1] = _dot(vt_scr[ib], jnp.exp2(s_own_b - m_b).astype(BF16))
    for t, (is_a, which, jb) in enumerate(trips):
        pr = jnp.exp2(s_scr[t] - jnp.where(is_a, m_a, m_b))
        acc_scr[which] += _dot(vt_scr[jb], pr.astype(BF16))
    for which, o_ref in ((0, oa_ref), (1, ob_ref)):
        acc = acc_scr[which]
        o_ref[...] = (acc[:hd] / acc[hd:hd + 1]).T.astype(BF16)


def _moba(proj, batch, seq):
    blk = MOBA_BLOCK
    nb = seq // blk
    assert nb % 2 == 0 and nb <= HEAD_DIM
    half = nb // 2
    H = MOBA_HEADS
    hd = HEAD_DIM
    width = H * hd
    lo, hi = pl.pallas_call(
        functools.partial(_moba_kernel, nb=nb, blk=blk, scale=hd ** -0.5 * LOG2_E),
        grid=(batch, H, half),
        in_specs=[
            pl.BlockSpec((blk, hd), lambda b, h, p: (b * nb + p, h)),
            pl.BlockSpec((blk, hd), lambda b, h, p: (b * nb + nb - 1 - p, h)),
            pl.BlockSpec((seq, hd), lambda b, h, p: (b, H + h)),
            pl.BlockSpec((seq, hd), lambda b, h, p: (b, 2 * H + h)),
        ],
        out_specs=[
            pl.BlockSpec((blk, hd), lambda b, h, p: (b * half + p, h)),
            pl.BlockSpec((blk, hd), lambda b, h, p: (b * half + half - 1 - p, h)),
        ],
        out_shape=[jax.ShapeDtypeStruct((batch * half * blk, width), BF16)] * 2,
        scratch_shapes=[
            pltpu.VMEM((nb, hd), F32),
            pltpu.VMEM((seq, 2 * hd), BF16),
            pltpu.VMEM((nb, hd + MOBA_ONES_ROWS, blk), BF16),
            pltpu.VMEM((2, 2 * hd, blk), BF16),
            pltpu.VMEM((nb - 1, blk, blk), F32),
            pltpu.VMEM((2, hd + MOBA_ONES_ROWS, blk), F32),
        ],
        compiler_params=pltpu.CompilerParams(
            dimension_semantics=("parallel", "parallel", "arbitrary"), vmem_limit_bytes=VMEM_LIMIT),
        name="moba",
    )(proj, proj, proj, proj)
    return lo, hi


def _gla_levels(c):
    out = []
    m = c // 2
    while m >= 1:
        out.append(m)
        m //= 2
    return out


def _gla_kernel(q_ref, k_ref, v_ref, g_ref, la_ref, nw_ref, o_ref, st_scr, *,
                chunk, group, heads, scale):
    C = chunk
    levels = _gla_levels(C)

    @pl.when(pl.program_id(2) == 0)
    def _():
        st_scr[...] = jnp.zeros_like(st_scr)

    t_io = lax.broadcasted_iota(I32, (C, C), 0)
    s_io = lax.broadcasted_iota(I32, (C, C), 1)
    sel_rows = [jnp.where(s_io <= t_io, 1.0, 0.0)]
    masks = []
    for m in levels:
        blk_t = t_io // (2 * m)
        bound = blk_t * (2 * m) + (m - 1)
        sel_rows.append(jnp.where(s_io <= bound, 1.0, 0.0))
        masks.append((blk_t == s_io // (2 * m)) & (t_io % (2 * m) >= m) & (s_io % (2 * m) < m))
    cum_mat = jnp.concatenate(sel_rows, axis=0).astype(BF16)
    diag = t_io == s_io
    nw = nw_ref[...]

    dk, dv = GLA_DK, GLA_DV
    for c in range(group):
        rows = slice(c * C, (c + 1) * C)
        for hh in range(heads):
            kc = slice(hh * dk, (hh + 1) * dk)
            vc = slice(hh * dv, (hh + 1) * dv)
            la_hi, la_lo = _split_bf16(la_ref[rows, kc])
            cums = _dot(cum_mat, la_hi) + _dot(cum_mat, la_lo)
            b = cums[:C]
            q = q_ref[rows, kc].astype(F32) * scale
            k = k_ref[rows, kc].astype(F32)
            v = v_ref[rows, vc]
            a = jnp.where(diag, _dot_nt(q.astype(BF16), k.astype(BF16)), 0.0)
            for li in range(len(levels)):
                e = jnp.exp(-jnp.abs(b - cums[(li + 1) * C:(li + 2) * C]))
                a = a + jnp.where(masks[li], _dot_nt((q * e).astype(BF16), (k * e).astype(BF16)), 0.0)
            st = st_scr[hh]
            o = _dot(a.astype(BF16), v) + _dot_nt((q * jnp.exp(b)).astype(BF16), st.astype(BF16))
            b_last = b[C - 1:C, :]
            ke = (k * jnp.exp(b_last - b)).astype(BF16)
            st_scr[hh] = st * jnp.exp(b_last) + _dot_tn(v, ke)
            gg = g_ref[rows, vc].astype(F32)
            o_ref[rows, vc] = (_rms(o, nw) * (gg * jax.nn.sigmoid(gg))).astype(BF16)


def _gla(proj, la, norm_w, batch, seq):
    T = proj.shape[0]
    Hg = GLA_HEADS
    rows = GLA_CHUNK * GLA_GROUP
    ng = seq // rows
    hs = GLA_HEADS_PER_STEP
    wk, wv = hs * GLA_DK, hs * GLA_DV
    q0 = (3 * MOBA_HEADS * HEAD_DIM) // wk
    k0 = q0 + Hg // hs
    v0 = (3 * MOBA_HEADS * HEAD_DIM + 2 * Hg * GLA_DK) // wv
    g0 = v0 + Hg // hs
    return pl.pallas_call(
        functools.partial(_gla_kernel, chunk=GLA_CHUNK, group=GLA_GROUP, heads=hs,
                          scale=GLA_DK ** -0.5),
        grid=(batch, Hg // hs, ng),
        in_specs=[
            pl.BlockSpec((rows, wk), lambda b, h, g: (b * ng + g, q0 + h)),
            pl.BlockSpec((rows, wk), lambda b, h, g: (b * ng + g, k0 + h)),
            pl.BlockSpec((rows, wv), lambda b, h, g: (b * ng + g, v0 + h)),
            pl.BlockSpec((rows, wv), lambda b, h, g: (b * ng + g, g0 + h)),
            pl.BlockSpec((rows, wk), lambda b, h, g: (b * ng + g, h)),
            pl.BlockSpec((1, GLA_DV), lambda b, h, g: (0, 0)),
        ],
        out_specs=pl.BlockSpec((rows, wv), lambda b, h, g: (b * ng + g, h)),
        out_shape=jax.ShapeDtypeStruct((T, Hg * GLA_DV), BF16),
        scratch_shapes=[pltpu.VMEM((hs, GLA_DV, GLA_DK), F32)],
        compiler_params=pltpu.CompilerParams(
            dimension_semantics=("parallel", "parallel", "arbitrary"), vmem_limit_bytes=VMEM_LIMIT),
        name="gla",
    )(proj, proj, proj, proj, la, norm_w)


def _split_bf16(x):
    hi = x.astype(BF16)
    return hi, (x - hi.astype(F32)).astype(BF16)


def _pack_bf16_pairs(x):
    n = x.shape[1] // 2
    lo = lax.bitcast_convert_type(x[:, :n].astype(BF16).astype(F32), U32)
    hi = lax.bitcast_convert_type(x[:, n:].astype(BF16).astype(F32), U32)
    return (lo >> 16) | (hi & jnp.uint32(0xFFFF0000))


def _unpack_bf16_pairs(p):
    lo = lax.bitcast_convert_type(p << 16, F32).astype(BF16)
    hi = lax.bitcast_convert_type(p & jnp.uint32(0xFFFF0000), F32).astype(BF16)
    return lo, hi


ROW_TILE = 8


def _token_tile(ref, t):
    return ref.at[pl.ds(pl.multiple_of(t * ROW_TILE, ROW_TILE), ROW_TILE), :]


def _store_token_rows(ref, x):
    n = x.shape[0]
    for c in range(ROW_TILE):
        ref[pl.ds(c, n, stride=ROW_TILE), :] = x[:, c * LANES:(c + 1) * LANES]


def _load_token_rows(ref, t0, n):
    base = pl.multiple_of(t0 * ROW_TILE, ROW_TILE)
    return jnp.concatenate(
        [ref[pl.ds(base + c, n, stride=ROW_TILE), :] for c in range(ROW_TILE)], axis=1)


def _out_proj_kernel(mlo_ref, mhi_ref, gl_ref, x_ref, wo_ref, nw_ref, rwh_ref, rwl_ref, rb_ref,
                     x1_ref, h2p_ref, idx_ref, gate_ref, mo_scr, *, seq_tiles):
    wm = mlo_ref.shape[1]
    first_half = (pl.program_id(0) % seq_tiles) < seq_tiles // 2
    @pl.when(first_half)
    def _():
        mo_scr[...] = mlo_ref[...]

    @pl.when(jnp.logical_not(first_half))
    def _():
        mo_scr[...] = mhi_ref[...]

    x1 = x_ref[...] + _dot(mo_scr[...], wo_ref[:wm, :]) + _dot(gl_ref[...], wo_ref[wm:, :])
    x1_ref[...] = x1
    h2 = _rms(x1, nw_ref[...])
    _store_token_rows(h2p_ref, _pack_bf16_pairs(h2))
    h_hi, h_lo = _split_bf16(h2)
    rw_hi = rwh_ref[...]
    logits = _dot_nt(rw_hi, h_hi) + (_dot_nt(rwl_ref[...], h_hi) + _dot_nt(rw_hi, h_lo))
    tm = logits.shape[1]
    logits = logits + jnp.concatenate([rb_ref[...]] * (tm // LANES), axis=1)
    row = lax.broadcasted_iota(I32, logits.shape, 0)
    vals, idxs = [], []
    for _ in range(TOP_K):
        mx = jnp.max(logits, axis=0, keepdims=True)
        ix = jnp.min(jnp.where(logits == mx, row, LANES), axis=0, keepdims=True)
        vals.append(mx)
        idxs.append(ix)
        logits = jnp.where(row == ix, -jnp.inf, logits)
    exps = [jnp.exp(v - vals[0]) for v in vals]
    denom = exps[0]
    for e in exps[1:]:
        denom = denom + e
    idx_t = jnp.zeros(logits.shape, I32)
    gate_t = jnp.zeros(logits.shape, F32)
    for r in range(TOP_K):
        idx_t = jnp.where(row == r, idxs[r], idx_t)
        gate_t = jnp.where(row == r, exps[r] / denom, gate_t)
    idx_ref[...] = idx_t[:idx_ref.shape[0], :]
    gate_ref[...] = gate_t.T


def _out_proj(moba_lo, moba_hi, gla_o, x2d, w_out, norm_w, rw_hi, rw_lo, rb_tile, seq):
    T, D = x2d.shape
    tm = PROJ_TM
    wm, wg = moba_lo.shape[1], gla_o.shape[1]
    st = seq // tm
    ht = st // 2
    assert st % 2 == 0

    def lo_map(i):
        return ((i // st) * ht + jnp.minimum(i % st, ht - 1), 0)

    def hi_map(i):
        return ((i // st) * ht + jnp.maximum(i % st - ht, 0), 0)

    return pl.pallas_call(
        functools.partial(_out_proj_kernel, seq_tiles=st),
        grid=(T // tm,),
        in_specs=[
            pl.BlockSpec((tm, wm), lo_map),
            pl.BlockSpec((tm, wm), hi_map),
            pl.BlockSpec((tm, wg), lambda i: (i, 0)),
            pl.BlockSpec((tm, D), lambda i: (i, 0)),
            pl.BlockSpec((wm + wg, D), lambda i: (0, 0)),
            pl.BlockSpec((1, D), lambda i: (0, 0)),
            pl.BlockSpec((LANES, D), lambda i: (0, 0)),
            pl.BlockSpec((LANES, D), lambda i: (0, 0)),
            pl.BlockSpec((LANES, LANES), lambda i: (0, 0)),
        ],
        out_specs=[
            pl.BlockSpec((tm, D), lambda i: (i, 0)),
            pl.BlockSpec((tm * ROW_TILE, LANES), lambda i: (i, 0)),
            pl.BlockSpec((ROW_TILE, tm), lambda i: (0, i)),
            pl.BlockSpec((tm, LANES), lambda i: (i, 0)),
        ],
        out_shape=[
            jax.ShapeDtypeStruct((T, D), F32),
            jax.ShapeDtypeStruct((T * ROW_TILE, LANES), U32),
            jax.ShapeDtypeStruct((ROW_TILE, T), I32),
            jax.ShapeDtypeStruct((T, LANES), F32),
        ],
        scratch_shapes=[pltpu.VMEM((tm, wm), BF16)],
        compiler_params=pltpu.CompilerParams(
            dimension_semantics=("parallel",), vmem_limit_bytes=VMEM_LIMIT),
        name="out_proj",
    )(moba_lo, moba_hi, gla_o, x2d, w_out, norm_w, rw_hi, rw_lo, rb_tile)


def _slot_bits(n_rows):
    return (n_rows - 1).bit_length()


def _moe_row_in(h_hbm, xbuf, sem, slot, tok, r):
    return pltpu.make_async_copy(_token_tile(h_hbm, tok), _token_tile(xbuf.at[slot], r), sem.at[slot])


def _moe_row_out(acc, y_hbm, sem, r, dst_row):
    return pltpu.make_async_copy(acc.at[pl.ds(r, 1), :], y_hbm.at[pl.ds(dst_row, 1), :], sem)


def _moe_blk_out(acc, y_hbm, sem, blk):
    dst = y_hbm.at[pl.ds(pl.multiple_of(blk * MOE_SUB, MOE_SUB), MOE_SUB), :]
    return pltpu.make_async_copy(acc.at[pl.ds(0, MOE_SUB), :], dst, sem)


def _moe_kernel(se_ref, rs_ref, ns_ref, asg_ref,
                h_hbm, wg_ref, wu_ref, wd_ref, bg_ref, bu_ref, bd_ref, y_hbm,
                xbuf, acc, gsem, osem, *, slot_bits, nf):
    s = pl.program_id(0)
    j = pl.program_id(1)
    n_grid = pl.num_programs(0)
    nsub = ns_ref[s]
    base = rs_ref[s] * MOE_SUB
    slot = s & 1
    per_trip = MOE_SUB // nf
    half = wg_ref.shape[1] // 2

    def src_token(p):
        return lax.shift_right_logical(asg_ref[p], slot_bits)

    def out_slot(p):
        return asg_ref[p] & ((1 << slot_bits) - 1)

    def wait_rows(n_sub_tiles):
        def wait(gidx, c):
            for u in range(MOE_DMA_UNROLL):
                _moe_row_in(h_hbm, xbuf, gsem, slot, 0, gidx * MOE_DMA_UNROLL + u).wait()
            return c

        lax.fori_loop(0, n_sub_tiles * (MOE_SUB // MOE_DMA_UNROLL), wait, 0)

    @pl.when(j == 0)
    def _():
        @pl.when(s == 0)
        def _():
            def issue(gidx, c):
                for u in range(MOE_DMA_UNROLL):
                    r = gidx * MOE_DMA_UNROLL + u
                    _moe_row_in(h_hbm, xbuf, gsem, slot, src_token(base + r), r).start()
                return c

            lax.fori_loop(0, nsub * (MOE_SUB // MOE_DMA_UNROLL), issue, 0)

        wait_rows(jnp.where(s == 0, nsub, ns_ref[jnp.maximum(s - 1, 0)]))

    @pl.when(nsub > 0)
    def _():
        @pl.when(j == 0)
        def _():
            bias_rows = jnp.broadcast_to(bd_ref[0], (MOE_SUB, acc.shape[1]))

            def init(r, c):
                acc[pl.ds(pl.multiple_of(r * MOE_SUB, MOE_SUB), MOE_SUB), :] = bias_rows
                return c

            lax.fori_loop(0, nsub, init, 0)

        bg = bg_ref[0]
        bu = bu_ref[0]
        next_base = rs_ref[s + 1] * MOE_SUB

        def tile(r, n, emit):
            rows = n * MOE_SUB
            off = pl.multiple_of(r * MOE_SUB, MOE_SUB)
            xl, xh = _unpack_bf16_pairs(_load_token_rows(xbuf.at[slot], off, rows))
            g = (_dot(xl, wg_ref[0, :half, :].astype(BF16))
                 + _dot(xh, wg_ref[0, half:, :].astype(BF16)) + bg)
            u = (_dot(xl, wu_ref[0, :half, :].astype(BF16))
                 + _dot(xh, wu_ref[0, half:, :].astype(BF16)) + bu)
            g = jnp.minimum(g, SWIGLU_LIMIT)
            u = jnp.clip(u, -SWIGLU_LIMIT, SWIGLU_LIMIT)
            act = (u + 1.0) * (g * jax.nn.sigmoid(SWIGLU_ALPHA * g))
            acc[pl.ds(off, rows), :] += _dot(act.astype(BF16), wd_ref[0].astype(BF16))
            n_in = n * per_trip
            g0 = pl.multiple_of((j * nsub + r) * per_trip, per_trip)
            x_next = xbuf.at[1 - slot, pl.ds(g0 * ROW_TILE, n_in * ROW_TILE), :]
            for v in range(n_in):
                tok = src_token(next_base + g0 + v)
                pltpu.make_async_copy(_token_tile(h_hbm, tok), _token_tile(x_next, v),
                                      gsem.at[1 - slot]).start()
            if emit:
                acc_rows = acc.at[pl.ds(off, rows), :]
                for v in range(rows):
                    _moe_row_out(acc_rows, y_hbm, osem, v, out_slot(base + off + v)).start()

        def sweep(emit):
            def pair(t, c):
                tile(2 * t, 2, emit)
                return c

            lax.fori_loop(0, nsub >> 1, pair, 0)

            @pl.when((nsub & 1) == 1)
            def _():
                tile(nsub - 1, 1, emit)

        @pl.when(j < nf - 1)
        def _():
            sweep(False)

        @pl.when(j == nf - 1)
        def _():
            sweep(True)

            def done(gidx, c):
                for u in range(MOE_DMA_UNROLL):
                    _moe_row_out(acc, y_hbm, osem, 0, 0).wait()
                return c

            lax.fori_loop(0, nsub * (MOE_SUB // MOE_DMA_UNROLL), done, 0)

    @pl.when((s == n_grid - 1) & (j == nf - 1))
    def _():
        tail0 = rs_ref[n_grid]
        n_blk = y_hbm.shape[0] // MOE_SUB
        acc[0:MOE_SUB, :] = jnp.zeros((MOE_SUB, acc.shape[1]), F32)

        def put(b, c):
            _moe_blk_out(acc, y_hbm, osem, b).start()
            return c

        lax.fori_loop(tail0, n_blk, put, 0)

        def done(b, c):
            _moe_blk_out(acc, y_hbm, osem, b).wait()
            return c

        lax.fori_loop(tail0, n_blk, done, 0)


def _moe(sup_e, sup_blk, sup_nsub, row_asg, h2p, w_gu, b_gu, w_dn, b_dn, n_rows):
    T = h2p.shape[0] // ROW_TILE
    D = 2 * ROW_TILE * LANES
    assert w_gu.shape[1] == D
    E, _, F2 = w_gu.shape
    F = F2 // 2
    tf = MOE_TF
    nf = F // tf
    assert MOE_SUB % nf == 0
    n_sup = sup_e.shape[0]

    def wmap(col0):
        def f(s, j, se, rs, ns, tok):
            return (se[s], 0, col0 + jnp.where(ns[s] > 0, j, nf - 1))
        return f

    def dmap(s, j, se, rs, ns, tok):
        return (se[s], jnp.where(ns[s] > 0, j, nf - 1), 0)

    def bdmap(s, j, se, rs, ns, tok):
        return (se[s], 0, 0)

    grid_spec = pltpu.PrefetchScalarGridSpec(
        num_scalar_prefetch=4,
        grid=(n_sup, nf),
        in_specs=[
            pl.BlockSpec(memory_space=pl.ANY),
            pl.BlockSpec((1, D, tf), wmap(0)),
            pl.BlockSpec((1, D, tf), wmap(nf)),
            pl.BlockSpec((1, tf, D), dmap),
            pl.BlockSpec((1, 1, tf), wmap(0)),
            pl.BlockSpec((1, 1, tf), wmap(nf)),
            pl.BlockSpec((1, 1, D), bdmap),
        ],
        out_specs=pl.BlockSpec(memory_space=pl.ANY),
        scratch_shapes=[
            pltpu.VMEM((2, MOE_SUP * ROW_TILE, LANES), U32),
            pltpu.VMEM((MOE_SUP, D), F32),
            pltpu.SemaphoreType.DMA((2,)),
            pltpu.SemaphoreType.DMA,
        ],
    )
    return pl.pallas_call(
        functools.partial(_moe_kernel, slot_bits=_slot_bits(n_rows), nf=nf),
        grid_spec=grid_spec,
        out_shape=jax.ShapeDtypeStruct((n_rows, D), F32),
        compiler_params=pltpu.CompilerParams(
            dimension_semantics=("arbitrary", "arbitrary"), vmem_limit_bytes=VMEM_LIMIT),
        name="moe",
    )(sup_e, sup_blk, sup_nsub, row_asg, h2p, w_gu, w_gu, w_dn,
      b_gu.reshape(E, 1, F2), b_gu.reshape(E, 1, F2), b_dn.reshape(E, 1, D))


def _combine_kernel(*refs):
    y_refs = refs[:TOP_K]
    x1_ref, gate_ref, nw_ref, o_ref = refs[TOP_K:]
    gates = gate_ref[...]
    x2 = x1_ref[...]
    for kk in range(TOP_K):
        x2 = x2 + gates[:, kk:kk + 1] * y_refs[kk][...]
    o_ref[...] = _rms(x2, nw_ref[...])


def _combine(y_slots, x1, gates, norm_w):
    T, D = x1.shape
    tm = COMB_TM
    tiles = T // tm

    def plane(kk):
        return pl.BlockSpec((tm, D), lambda i: (kk * tiles + i, 0))

    return pl.pallas_call(
        _combine_kernel,
        grid=(tiles,),
        in_specs=[plane(kk) for kk in range(TOP_K)] + [
            pl.BlockSpec((tm, D), lambda i: (i, 0)),
            pl.BlockSpec((tm, LANES), lambda i: (i, 0)),
            pl.BlockSpec((1, D), lambda i: (0, 0)),
        ],
        out_specs=pl.BlockSpec((tm, D), lambda i: (i, 0)),
        out_shape=jax.ShapeDtypeStruct((T, D), F32),
        compiler_params=pltpu.CompilerParams(
            dimension_semantics=("parallel",), vmem_limit_bytes=VMEM_LIMIT),
        name="combine",
    )(*([y_slots] * TOP_K), x1, gates, norm_w)


def _route(top_idx_t, n_experts):
    T = top_idx_t.shape[1]
    A = T * TOP_K
    G, SUP = MOE_SUB, MOE_SUP
    flat_e = top_idx_t.reshape(A)
    onehot = (flat_e[:, None] == jnp.arange(n_experts, dtype=I32)[None, :]).astype(I32)
    csum = jnp.cumsum(onehot, axis=0)
    counts = csum[-1]
    rank = jnp.sum((csum - 1) * onehot, axis=1)
    pcounts = (counts + G - 1) // G * G
    pends = jnp.cumsum(pcounts)
    pstarts = pends - pcounts
    dest = (pstarts[flat_e] + rank).astype(I32)
    n_rows = (A + n_experts * (G - 1)) // G * G
    p_ids = jnp.arange(n_rows, dtype=I32)
    seg = jnp.minimum(jnp.sum((p_ids[:, None] >= pends[None, :]).astype(I32), axis=1), n_experts - 1)
    pad_ids = A + p_ids - jnp.cumsum(counts)[seg]
    a_ids = jnp.arange(A, dtype=I32)
    bits = _slot_bits(n_rows)
    assert (T - 1) << bits < 2 ** 31
    row_asg = ((T - 1) << bits | pad_ids).astype(I32).at[dest].set((a_ids % T) << bits | a_ids)
    row_asg = jnp.concatenate([row_asg, jnp.zeros((SUP,), I32)])

    n_sup = (A + n_experts * (G - 1) + n_experts * (SUP - G)) // SUP + 1
    sub_e = pcounts // G
    nsup_e = (pcounts + SUP - 1) // SUP
    sup_ends = jnp.cumsum(nsup_e)
    total = sup_ends[-1]
    s_ids = jnp.arange(n_sup, dtype=I32)
    s_eff = jnp.minimum(s_ids, total - 1)
    e_of = jnp.sum((s_eff[:, None] >= sup_ends[None, :]).astype(I32), axis=1)
    local = s_eff - (sup_ends - nsup_e)[e_of]
    active = s_ids < total
    n_e = jnp.maximum(nsup_e[e_of], 1)
    small, extra = sub_e[e_of] // n_e, sub_e[e_of] % n_e
    blk0 = jnp.where(active, pstarts[e_of] // G + local * small + jnp.minimum(local, extra), 0)
    nsub = jnp.where(active, small + (local < extra).astype(I32), 0)
    order = jnp.argsort(-nsub, stable=True)
    e_sorted = e_of[order]
    sup_e = jnp.where(active, e_sorted, e_sorted[total - 1]).astype(I32)
    sup_blk = jnp.concatenate([blk0[order], pends[-1:] // G]).astype(I32)
    sup_nsub = nsub[order].astype(I32)
    return row_asg, n_rows, sup_e, sup_blk, sup_nsub


def _rope_tables(seq):
    half = HEAD_DIM // 2
    inv_freq = 1.0 / (ROPE_THETA ** (jnp.arange(0, HEAD_DIM, 2, dtype=F32) / HEAD_DIM))
    ang = jnp.arange(seq, dtype=F32)[:, None] * inv_freq[None, :]
    cos, sin = jnp.cos(ang), jnp.sin(ang)
    assert cos.shape[1] == half
    return jnp.concatenate([cos, cos], axis=1), jnp.concatenate([-sin, sin], axis=1)


def kernel(x, attn_norm_w, w_in, gla_gate_w2, gla_gate_b, gla_norm_w, w_out, ffn_norm_w,
           router_w, router_b, w_gate_up, b_gate_up, w_down, b_down, final_norm_w):
    B, S, D = x.shape
    T = B * S
    assert w_in.shape[0] == 1, "single-layer stack: the final norm is fused into combine"
    n_experts = router_w.shape[-1]
    n_main = w_in.shape[-1] - GLA_GATE_RANK
    cosf, sinf = _rope_tables(S)
    x2d = x.reshape(T, D)
    w_main = w_in[0, :, :n_main].astype(BF16)
    w_gz = jnp.pad(w_in[0, :, n_main:], ((0, 0), (0, LANES - GLA_GATE_RANK))).astype(BF16)
    w2_pad = jnp.pad(gla_gate_w2[0], ((0, LANES - GLA_GATE_RANK), (0, 0)))
    proj, log_a = _in_proj(x2d, attn_norm_w[0][None, :], w_main, w_gz, w2_pad,
                           gla_gate_b[0][None, :], cosf, sinf, S)
    moba_lo, moba_hi = _moba(proj, B, S)
    gla_o = _gla(proj, log_a, gla_norm_w[0][None, :], B, S)
    rw_t = jnp.pad(router_w[0].T, ((0, LANES - n_experts), (0, 0)))
    rw_hi = rw_t.astype(BF16)
    rw_lo = (rw_t - rw_hi.astype(F32)).astype(BF16)
    rb_col = jnp.pad(router_b[0], (0, LANES - n_experts), constant_values=NEG)
    rb_tile = jnp.broadcast_to(rb_col[:, None], (LANES, LANES))
    x1, h2p, top_idx_t, gates = _out_proj(moba_lo, moba_hi, gla_o, x2d, w_out[0].astype(BF16),
                                          ffn_norm_w[0][None, :], rw_hi, rw_lo, rb_tile, S)
    row_asg, n_rows, sup_e, sup_blk, sup_nsub = _route(top_idx_t[:TOP_K], n_experts)
    y_slots = _moe(sup_e, sup_blk, sup_nsub, row_asg, h2p, w_gate_up[0], b_gate_up[0],
                   w_down[0], b_down[0], n_rows)
    out = _combine(y_slots, x1, gates, final_norm_w[None, :])
    return out.reshape(B, S, D)
```

```python
import functools

import jax
import jax.numpy as jnp
from jax import lax
from jax.experimental import pallas as pl
from jax.experimental.pallas import tpu as pltpu

F32 = jnp.float32
BF16 = jnp.bfloat16
I32 = jnp.int32
U32 = jnp.uint32

HEAD_DIM = 128
MOBA_HEADS = 8
MOBA_BLOCK = 256
MOBA_TOPK = 3
ROPE_THETA = 10000.0
GLA_HEADS = 4
GLA_DK = 128
GLA_DV = 256
GLA_GATE_RANK = 16
GLA_GATE_NORMALIZER = 16.0
TOP_K = 4
SWIGLU_ALPHA = 1.702
SWIGLU_LIMIT = 7.0
NORM_EPS = 1e-5

LANES = 128
VMEM_LIMIT = 58 * 1024 * 1024
NEG = -1e30
LOG2_E = 1.4426950408889634

IN_PROJ_TM = 1024
PROJ_TM = 512
PROJ_TN = 1024
GLA_CHUNK = 64
GLA_GROUP = 16
GLA_HEADS_PER_STEP = 2
MOE_SUB = 256
MOE_SUP = 2048
MOE_TF = 256
MOE_DMA_UNROLL = 8
COMB_TM = 256


def _dot(a, b, **kw):
    return jnp.dot(a, b, preferred_element_type=F32, **kw)


def _dot_nt(a, b, **kw):
    return lax.dot_general(a, b, (((1,), (1,)), ((), ())), preferred_element_type=F32, **kw)


def _dot_tn(a, b, **kw):
    return lax.dot_general(a, b, (((0,), (0,)), ((), ())), preferred_element_type=F32, **kw)


def _rms(x, w):
    return x * lax.rsqrt(jnp.mean(x * x, axis=-1, keepdims=True) + NORM_EPS) * w


def _in_proj_kernel(x_ref, nw_ref, w_ref, wz_ref, w2_ref, gb_ref, cos_ref, sin_ref,
                    proj_ref, la_ref, h_scr, *, n_main, n_rope):
    j = pl.program_id(1)

    @pl.when(j == 0)
    def _():
        h_scr[...] = _rms(x_ref[...], nw_ref[...]).astype(BF16)

    @pl.when(j < n_rope)
    def _():
        acc = _dot(h_scr[...], w_ref[...])
        cosf = cos_ref[...]
        sinf = sin_ref[...]
        for hd in range(acc.shape[1] // HEAD_DIM):
            t = acc[:, hd * HEAD_DIM:(hd + 1) * HEAD_DIM]
            r = t * cosf + pltpu.roll(t, HEAD_DIM // 2, axis=1) * sinf
            proj_ref[:, hd * HEAD_DIM:(hd + 1) * HEAD_DIM] = r.astype(BF16)

    @pl.when((j >= n_rope) & (j < n_main))
    def _():
        proj_ref[...] = _dot(h_scr[...], w_ref[...]).astype(BF16)

    @pl.when(j == n_main)
    def _():
        gz = _dot(h_scr[...], wz_ref[...])
        gz_hi, gz_lo = _split_bf16(gz)
        w2_hi, w2_lo = _split_bf16(w2_ref[...])
        logits = _dot(gz_hi, w2_hi) + (_dot(gz_hi, w2_lo) + _dot(gz_lo, w2_hi)) + gb_ref[...]
        logsig = jnp.minimum(logits, 0.0) - jnp.log1p(jnp.exp(-jnp.abs(logits)))
        la_ref[...] = logsig * (1.0 / GLA_GATE_NORMALIZER)


def _in_proj(x2d, norm_w, w_main, w_gz, w2_pad, gate_b, cosf, sinf, seq):
    T, D = x2d.shape
    n_cols = w_main.shape[1]
    tm, tn = IN_PROJ_TM, PROJ_TN
    n_main = n_cols // tn
    n_rope = (2 * MOBA_HEADS * HEAD_DIM) // tn
    kw = w2_pad.shape[1]
    s_tiles = seq // tm
    last = n_main - 1
    return pl.pallas_call(
        functools.partial(_in_proj_kernel, n_main=n_main, n_rope=n_rope),
        grid=(T // tm, n_main + 1),
        in_specs=[
            pl.BlockSpec((tm, D), lambda i, j: (i, 0)),
            pl.BlockSpec((1, D), lambda i, j: (0, 0)),
            pl.BlockSpec((D, tn), lambda i, j: (0, jnp.minimum(j, last))),
            pl.BlockSpec((D, LANES), lambda i, j: (0, 0)),
            pl.BlockSpec((LANES, kw), lambda i, j: (0, 0)),
            pl.BlockSpec((1, kw), lambda i, j: (0, 0)),
            pl.BlockSpec((tm, HEAD_DIM), lambda i, j: (i % s_tiles, 0)),
            pl.BlockSpec((tm, HEAD_DIM), lambda i, j: (i % s_tiles, 0)),
        ],
        out_specs=[
            pl.BlockSpec((tm, tn), lambda i, j: (i, jnp.minimum(j, last))),
            pl.BlockSpec((tm, kw), lambda i, j: (i, 0)),
        ],
        out_shape=[
            jax.ShapeDtypeStruct((T, n_cols), BF16),
            jax.ShapeDtypeStruct((T, kw), F32),
        ],
        scratch_shapes=[pltpu.VMEM((tm, D), BF16)],
        compiler_params=pltpu.CompilerParams(
            dimension_semantics=("parallel", "arbitrary"), vmem_limit_bytes=VMEM_LIMIT),
        name="in_proj",
    )(x2d, norm_w, w_main, w_gz, w2_pad, gate_b, cosf, sinf)


MOBA_ONES_ROWS = 16


def _moba_select(q_ref, kmean_scr, qs_scr, which, iq, *, nb, blk, scale):
    hd = HEAD_DIM
    qt = q_ref[...].astype(F32).T
    qt_b = qt.astype(BF16)
    km = kmean_scr[...]
    km_hi, km_r = km.astype(BF16), km - km.astype(BF16).astype(F32)
    km_mid, km_lo = _split_bf16(km_r)
    gate = _dot(km_hi, qt_b) + (_dot(km_mid, qt_b) + _dot(km_lo, qt_b))
    n_iota = lax.broadcasted_iota(I32, (nb, blk), 0)
    rank = jnp.zeros((nb, blk), I32)
    for m in range(nb):
        gm = gate[m:m + 1, :]
        beats = (gm > gate) | ((gm == gate) & (m < n_iota))
        rank = rank + jnp.where(beats & (m < iq), 1, 0)
    keep = ((n_iota < iq) & (rank < MOBA_TOPK)) | (n_iota == iq)
    maskv = jnp.where(keep, 0.0, NEG)
    maskv = jnp.concatenate([maskv, jnp.zeros((hd - nb, blk), F32)], axis=0)
    qs_scr[which, :hd, :] = (qt * scale).astype(BF16)
    qs_scr[which, hd:, :] = maskv.astype(BF16)


def _moba_kernel(qa_ref, qb_ref, k_ref, v_ref, oa_ref, ob_ref,
                 kmean_scr, kaug_scr, vt_scr, qs_scr, s_scr, acc_scr, *, nb, blk, scale):
    p = pl.program_id(2)
    hd = HEAD_DIM
    sub = 8

    @pl.when(p == 0)
    def _():
        col = lax.broadcasted_iota(I32, (blk, hd), 1)
        ones = jnp.ones((MOBA_ONES_ROWS, blk), BF16)
        for n in range(nb):
            rows = slice(n * blk, (n + 1) * blk)
            kb = k_ref[rows, :]
            kmean_scr[n:n + 1, :] = jnp.sum(kb.astype(F32), axis=0, keepdims=True) * (1.0 / blk)
            kaug_scr[rows, :hd] = kb
            kaug_scr[rows, hd:] = jnp.where(col == n, 1.0, 0.0).astype(BF16)
            vt_scr[n, :hd, :] = v_ref[rows, :].astype(F32).T.astype(BF16)
            vt_scr[n, hd:, :] = ones

    ia = p
    ib = nb - 1 - p
    _moba_select(qa_ref, kmean_scr, qs_scr, 0, ia, nb=nb, blk=blk, scale=scale)
    _moba_select(qb_ref, kmean_scr, qs_scr, 1, ib, nb=nb, blk=blk, scale=scale)

    def fold_max(x):
        return jnp.max(x.reshape(blk // sub, sub, blk), axis=0)

    def key_rows(jb):
        return kaug_scr[pl.ds(pl.multiple_of(jb * blk, blk), blk), :]

    kpos = lax.broadcasted_iota(I32, (blk, blk), 0)
    qpos = lax.broadcasted_iota(I32, (blk, blk), 1)
    causal = jnp.where(kpos <= qpos, 0.0, NEG)

    s_own_a = _dot(key_rows(ia), qs_scr[0]) + causal
    s_own_b = _dot(key_rows(ib), qs_scr[1]) + causal
    mx_a = fold_max(s_own_a)
    mx_b = fold_max(s_own_b)
    trips = []
    for t in range(nb - 1):
        is_a = t < p
        which = jnp.where(is_a, 0, 1)
        jb = jnp.where(is_a, t, t - p)
        trips.append((is_a, which, jb))
        s = _dot(key_rows(jb), qs_scr[which])
        s_scr[t] = s
        f = fold_max(s)
        mx_a = jnp.maximum(mx_a, jnp.where(is_a, f, NEG))
        mx_b = jnp.maximum(mx_b, jnp.where(is_a, NEG, f))
    m_a = jnp.max(mx_a, axis=0, keepdims=True)
    m_b = jnp.max(mx_b, axis=0, keepdims=True)

    acc_scr[0] = _dot(vt_scr[ia], jnp.exp2(s_own_a - m_a).astype(BF16))
    acc_scr[1] = _dot(vt_scr[ib], jnp.exp2(s_own_b - m_b).astype(BF16))
    for t, (is_a, which, jb) in enumerate(trips):
        pr = jnp.exp2(s_scr[t] - jnp.where(is_a, m_a, m_b))
        da = _dot(vt_scr[jb], pr.astype(BF16))
        acc_scr[0] += jnp.where(is_a, da, 0.0)
        acc_scr[1] += jnp.where(is_a, 0.0, da)
    for which, o_ref in ((0, oa_ref), (1, ob_ref)):
        acc = acc_scr[which]
        o_ref[...] = (acc[:hd] / acc[hd:hd + 1]).T.astype(BF16)


def _moba(proj, batch, seq):
    blk = MOBA_BLOCK
    nb = seq // blk
    assert nb % 2 == 0 and nb <= HEAD_DIM
    half = nb // 2
    H = MOBA_HEADS
    hd = HEAD_DIM
    width = H * hd
    lo, hi = pl.pallas_call(
        functools.partial(_moba_kernel, nb=nb, blk=blk, scale=hd ** -0.5 * LOG2_E),
        grid=(batch, H, half),
        in_specs=[
            pl.BlockSpec((blk, hd), lambda b, h, p: (b * nb + p, h)),
            pl.BlockSpec((blk, hd), lambda b, h, p: (b * nb + nb - 1 - p, h)),
            pl.BlockSpec((seq, hd), lambda b, h, p: (b, H + h)),
            pl.BlockSpec((seq, hd), lambda b, h, p: (b, 2 * H + h)),
        ],
        out_specs=[
            pl.BlockSpec((blk, hd), lambda b, h, p: (b * half + p, h)),
            pl.BlockSpec((blk, hd), lambda b, h, p: (b * half + half - 1 - p, h)),
        ],
        out_shape=[jax.ShapeDtypeStruct((batch * half * blk, width), BF16)] * 2,
        scratch_shapes=[
            pltpu.VMEM((nb, hd), F32),
            pltpu.VMEM((seq, 2 * hd), BF16),
            pltpu.VMEM((nb, hd + MOBA_ONES_ROWS, blk), BF16),
            pltpu.VMEM((2, 2 * hd, blk), BF16),
            pltpu.VMEM((nb - 1, blk, blk), F32),
            pltpu.VMEM((2, hd + MOBA_ONES_ROWS, blk), F32),
        ],
        compiler_params=pltpu.CompilerParams(
            dimension_semantics=("parallel", "parallel", "arbitrary"), vmem_limit_bytes=VMEM_LIMIT),
        name="moba",
    )(proj, proj, proj, proj)
    return lo, hi


def _gla_levels(c):
    out = []
    m = c // 2
    while m >= 1:
        out.append(m)
        m //= 2
    return out


def _gla_kernel(q_ref, k_ref, v_ref, g_ref, la_ref, nw_ref, o_ref, st_scr, *,
                chunk, group, heads, scale):
    C = chunk
    levels = _gla_levels(C)

    @pl.when(pl.program_id(2) == 0)
    def _():
        st_scr[...] = jnp.zeros_like(st_scr)

    t_io = lax.broadcasted_iota(I32, (C, C), 0)
    s_io = lax.broadcasted_iota(I32, (C, C), 1)
    sel_rows = [jnp.where(s_io <= t_io, 1.0, 0.0)]
    masks = []
    for m in levels:
        blk_t = t_io // (2 * m)
        bound = blk_t * (2 * m) + (m - 1)
        sel_rows.append(jnp.where(s_io <= bound, 1.0, 0.0))
        masks.append((blk_t == s_io // (2 * m)) & (t_io % (2 * m) >= m) & (s_io % (2 * m) < m))
    cum_mat = jnp.concatenate(sel_rows, axis=0).astype(BF16)
    diag = t_io == s_io
    nw = nw_ref[...]

    dk, dv = GLA_DK, GLA_DV
    for c in range(group):
        rows = slice(c * C, (c + 1) * C)
        for hh in range(heads):
            kc = slice(hh * dk, (hh + 1) * dk)
            vc = slice(hh * dv, (hh + 1) * dv)
            la_hi, la_lo = _split_bf16(la_ref[rows, kc])
            cums = _dot(cum_mat, la_hi) + _dot(cum_mat, la_lo)
            b = cums[:C]
            q = q_ref[rows, kc].astype(F32) * scale
            k = k_ref[rows, kc].astype(F32)
            v = v_ref[rows, vc]
            a = jnp.where(diag, _dot_nt(q.astype(BF16), k.astype(BF16)), 0.0)
            for li in range(len(levels)):
                e = jnp.exp(-jnp.abs(b - cums[(li + 1) * C:(li + 2) * C]))
                a = a + jnp.where(masks[li], _dot_nt((q * e).astype(BF16), (k * e).astype(BF16)), 0.0)
            st = st_scr[hh]
            o = _dot(a.astype(BF16), v) + _dot_nt((q * jnp.exp(b)).astype(BF16), st.astype(BF16))
            b_last = b[C - 1:C, :]
            ke = (k * jnp.exp(b_last - b)).astype(BF16)
            st_scr[hh] = st * jnp.exp(b_last) + _dot_tn(v, ke)
            gg = g_ref[rows, vc].astype(F32)
            o_ref[rows, vc] = (_rms(o, nw) * (gg * jax.nn.sigmoid(gg))).astype(BF16)


def _gla(proj, la, norm_w, batch, seq):
    T = proj.shape[0]
    Hg = GLA_HEADS
    rows = GLA_CHUNK * GLA_GROUP
    ng = seq // rows
    hs = GLA_HEADS_PER_STEP
    wk, wv = hs * GLA_DK, hs * GLA_DV
    q0 = (3 * MOBA_HEADS * HEAD_DIM) // wk
    k0 = q0 + Hg // hs
    v0 = (3 * MOBA_HEADS * HEAD_DIM + 2 * Hg * GLA_DK) // wv
    g0 = v0 + Hg // hs
    return pl.pallas_call(
        functools.partial(_gla_kernel, chunk=GLA_CHUNK, group=GLA_GROUP, heads=hs,
                          scale=GLA_DK ** -0.5),
        grid=(batch, Hg // hs, ng),
        in_specs=[
            pl.BlockSpec((rows, wk), lambda b, h, g: (b * ng + g, q0 + h)),
            pl.BlockSpec((rows, wk), lambda b, h, g: (b * ng + g, k0 + h)),
            pl.BlockSpec((rows, wv), lambda b, h, g: (b * ng + g, v0 + h)),
            pl.BlockSpec((rows, wv), lambda b, h, g: (b * ng + g, g0 + h)),
            pl.BlockSpec((rows, wk), lambda b, h, g: (b * ng + g, h)),
            pl.BlockSpec((1, GLA_DV), lambda b, h, g: (0, 0)),
        ],
        out_specs=pl.BlockSpec((rows, wv), lambda b, h, g: (b * ng + g, h)),
        out_shape=jax.ShapeDtypeStruct((T, Hg * GLA_DV), BF16),
        scratch_shapes=[pltpu.VMEM((hs, GLA_DV, GLA_DK), F32)],
        compiler_params=pltpu.CompilerParams(
            dimension_semantics=("parallel", "parallel", "arbitrary"), vmem_limit_bytes=VMEM_LIMIT),
        name="gla",
    )(proj, proj, proj, proj, la, norm_w)


def _split_bf16(x):
    hi = x.astype(BF16)
    return hi, (x - hi.astype(F32)).astype(BF16)


def _pack_bf16_pairs(x):
    n = x.shape[1] // 2
    lo = lax.bitcast_convert_type(x[:, :n].astype(BF16).astype(F32), U32)
    hi = lax.bitcast_convert_type(x[:, n:].astype(BF16).astype(F32), U32)
    return (lo >> 16) | (hi & jnp.uint32(0xFFFF0000))


def _unpack_bf16_pairs(p):
    lo = lax.bitcast_convert_type(p << 16, F32).astype(BF16)
    hi = lax.bitcast_convert_type(p & jnp.uint32(0xFFFF0000), F32).astype(BF16)
    return lo, hi


ROW_TILE = 8


def _token_tile(ref, t):
    return ref.at[pl.ds(pl.multiple_of(t * ROW_TILE, ROW_TILE), ROW_TILE), :]


def _store_token_rows(ref, x):
    n = x.shape[0]
    for c in range(ROW_TILE):
        ref[pl.ds(c, n, stride=ROW_TILE), :] = x[:, c * LANES:(c + 1) * LANES]


def _load_token_rows(ref, t0, n):
    base = pl.multiple_of(t0 * ROW_TILE, ROW_TILE)
    return jnp.concatenate(
        [ref[pl.ds(base + c, n, stride=ROW_TILE), :] for c in range(ROW_TILE)], axis=1)


def _out_proj_kernel(mlo_ref, mhi_ref, gl_ref, x_ref, wo_ref, nw_ref, rwh_ref, rwl_ref, rb_ref,
                     x1_ref, h2p_ref, idx_ref, gate_ref, mo_scr, *, seq_tiles):
    wm = mlo_ref.shape[1]
    first_half = (pl.program_id(0) % seq_tiles) < seq_tiles // 2
    @pl.when(first_half)
    def _():
        mo_scr[...] = mlo_ref[...]

    @pl.when(jnp.logical_not(first_half))
    def _():
        mo_scr[...] = mhi_ref[...]

    x1 = x_ref[...] + _dot(mo_scr[...], wo_ref[:wm, :]) + _dot(gl_ref[...], wo_ref[wm:, :])
    x1_ref[...] = x1
    h2 = _rms(x1, nw_ref[...])
    _store_token_rows(h2p_ref, _pack_bf16_pairs(h2))
    h_hi, h_lo = _split_bf16(h2)
    rw_hi = rwh_ref[...]
    logits = _dot_nt(rw_hi, h_hi) + (_dot_nt(rwl_ref[...], h_hi) + _dot_nt(rw_hi, h_lo))
    tm = logits.shape[1]
    logits = logits + jnp.concatenate([rb_ref[...]] * (tm // LANES), axis=1)
    row = lax.broadcasted_iota(I32, logits.shape, 0)
    vals, idxs = [], []
    for _ in range(TOP_K):
        mx = jnp.max(logits, axis=0, keepdims=True)
        ix = jnp.min(jnp.where(logits == mx, row, LANES), axis=0, keepdims=True)
        vals.append(mx)
        idxs.append(ix)
        logits = jnp.where(row == ix, -jnp.inf, logits)
    exps = [jnp.exp(v - vals[0]) for v in vals]
    denom = exps[0]
    for e in exps[1:]:
        denom = denom + e
    idx_t = jnp.zeros(logits.shape, I32)
    gate_t = jnp.zeros(logits.shape, F32)
    for r in range(TOP_K):
        idx_t = jnp.where(row == r, idxs[r], idx_t)
        gate_t = jnp.where(row == r, exps[r] / denom, gate_t)
    idx_ref[...] = idx_t[:idx_ref.shape[0], :]
    gate_ref[...] = gate_t.T


def _out_proj(moba_lo, moba_hi, gla_o, x2d, w_out, norm_w, rw_hi, rw_lo, rb_tile, seq):
    T, D = x2d.shape
    tm = PROJ_TM
    wm, wg = moba_lo.shape[1], gla_o.shape[1]
    st = seq // tm
    ht = st // 2
    assert st % 2 == 0

    def lo_map(i):
        return ((i // st) * ht + jnp.minimum(i % st, ht - 1), 0)

    def hi_map(i):
        return ((i // st) * ht + jnp.maximum(i % st - ht, 0), 0)

    return pl.pallas_call(
        functools.partial(_out_proj_kernel, seq_tiles=st),
        grid=(T // tm,),
        in_specs=[
            pl.BlockSpec((tm, wm), lo_map),
            pl.BlockSpec((tm, wm), hi_map),
            pl.BlockSpec((tm, wg), lambda i: (i, 0)),
            pl.BlockSpec((tm, D), lambda i: (i, 0)),
            pl.BlockSpec((wm + wg, D), lambda i: (0, 0)),
            pl.BlockSpec((1, D), lambda i: (0, 0)),
            pl.BlockSpec((LANES, D), lambda i: (0, 0)),
            pl.BlockSpec((LANES, D), lambda i: (0, 0)),
            pl.BlockSpec((LANES, LANES), lambda i: (0, 0)),
        ],
        out_specs=[
            pl.BlockSpec((tm, D), lambda i: (i, 0)),
            pl.BlockSpec((tm * ROW_TILE, LANES), lambda i: (i, 0)),
            pl.BlockSpec((ROW_TILE, tm), lambda i: (0, i)),
            pl.BlockSpec((tm, LANES), lambda i: (i, 0)),
        ],
        out_shape=[
            jax.ShapeDtypeStruct((T, D), F32),
            jax.ShapeDtypeStruct((T * ROW_TILE, LANES), U32),
            jax.ShapeDtypeStruct((ROW_TILE, T), I32),
            jax.ShapeDtypeStruct((T, LANES), F32),
        ],
        scratch_shapes=[pltpu.VMEM((tm, wm), BF16)],
        compiler_params=pltpu.CompilerParams(
            dimension_semantics=("parallel",), vmem_limit_bytes=VMEM_LIMIT),
        name="out_proj",
    )(moba_lo, moba_hi, gla_o, x2d, w_out, norm_w, rw_hi, rw_lo, rb_tile)


def _slot_bits(n_rows):
    return (n_rows - 1).bit_length()


def _moe_row_in(h_hbm, xbuf, sem, slot, tok, r):
    return pltpu.make_async_copy(_token_tile(h_hbm, tok), _token_tile(xbuf.at[slot], r), sem.at[slot])


def _moe_row_out(acc, y_hbm, sem, r, dst_row):
    return pltpu.make_async_copy(acc.at[pl.ds(r, 1), :], y_hbm.at[pl.ds(dst_row, 1), :], sem)


def _moe_blk_out(acc, y_hbm, sem, blk):
    dst = y_hbm.at[pl.ds(pl.multiple_of(blk * MOE_SUB, MOE_SUB), MOE_SUB), :]
    return pltpu.make_async_copy(acc.at[pl.ds(0, MOE_SUB), :], dst, sem)


def _moe_kernel(se_ref, rs_ref, ns_ref, asg_ref,
                h_hbm, wg_ref, wu_ref, wd_ref, bg_ref, bu_ref, bd_ref, y_hbm,
                xbuf, acc, gsem, osem, *, slot_bits, nf):
    s = pl.program_id(0)
    j = pl.program_id(1)
    n_grid = pl.num_programs(0)
    nsub = ns_ref[s]
    base = rs_ref[s] * MOE_SUB
    slot = s & 1
    per_trip = MOE_SUB // nf
    half = wg_ref.shape[1] // 2

    def src_token(p):
        return lax.shift_right_logical(asg_ref[p], slot_bits)

    def out_slot(p):
        return asg_ref[p] & ((1 << slot_bits) - 1)

    def wait_rows(n_sub_tiles):
        def wait(gidx, c):
            for u in range(MOE_DMA_UNROLL):
                _moe_row_in(h_hbm, xbuf, gsem, slot, 0, gidx * MOE_DMA_UNROLL + u).wait()
            return c

        lax.fori_loop(0, n_sub_tiles * (MOE_SUB // MOE_DMA_UNROLL), wait, 0)

    @pl.when(j == 0)
    def _():
        @pl.when(s == 0)
        def _():
            def issue(gidx, c):
                for u in range(MOE_DMA_UNROLL):
                    r = gidx * MOE_DMA_UNROLL + u
                    _moe_row_in(h_hbm, xbuf, gsem, slot, src_token(base + r), r).start()
                return c

            lax.fori_loop(0, nsub * (MOE_SUB // MOE_DMA_UNROLL), issue, 0)

        wait_rows(jnp.where(s == 0, nsub, ns_ref[jnp.maximum(s - 1, 0)]))

    @pl.when(nsub > 0)
    def _():
        @pl.when(j == 0)
        def _():
            bias_rows = jnp.broadcast_to(bd_ref[0], (MOE_SUB, acc.shape[1]))

            def init(r, c):
                acc[pl.ds(pl.multiple_of(r * MOE_SUB, MOE_SUB), MOE_SUB), :] = bias_rows
                return c

            lax.fori_loop(0, nsub, init, 0)

        bg = bg_ref[0]
        bu = bu_ref[0]
        next_base = rs_ref[s + 1] * MOE_SUB

        def tile(r, n, emit):
            rows = n * MOE_SUB
            off = pl.multiple_of(r * MOE_SUB, MOE_SUB)
            xl, xh = _unpack_bf16_pairs(_load_token_rows(xbuf.at[slot], off, rows))
            g = (_dot(xl, wg_ref[0, :half, :].astype(BF16))
                 + _dot(xh, wg_ref[0, half:, :].astype(BF16)) + bg)
            u = (_dot(xl, wu_ref[0, :half, :].astype(BF16))
                 + _dot(xh, wu_ref[0, half:, :].astype(BF16)) + bu)
            g = jnp.minimum(g, SWIGLU_LIMIT)
            u = jnp.clip(u, -SWIGLU_LIMIT, SWIGLU_LIMIT)
            act = (u + 1.0) * (g * jax.nn.sigmoid(SWIGLU_ALPHA * g))
            acc[pl.ds(off, rows), :] += _dot(act.astype(BF16), wd_ref[0].astype(BF16))
            n_in = n * per_trip
            g0 = pl.multiple_of((j * nsub + r) * per_trip, per_trip)
            x_next = xbuf.at[1 - slot, pl.ds(g0 * ROW_TILE, n_in * ROW_TILE), :]
            for v in range(n_in):
                tok = src_token(next_base + g0 + v)
                pltpu.make_async_copy(_token_tile(h_hbm, tok), _token_tile(x_next, v),
                                      gsem.at[1 - slot]).start()
            if emit:
                acc_rows = acc.at[pl.ds(off, rows), :]
                for v in range(rows):
                    _moe_row_out(acc_rows, y_hbm, osem, v, out_slot(base + off + v)).start()

        def sweep(emit):
            def pair(t, c):
                tile(2 * t, 2, emit)
                return c

            lax.fori_loop(0, nsub >> 1, pair, 0)

            @pl.when((nsub & 1) == 1)
            def _():
                tile(nsub - 1, 1, emit)

        @pl.when(j < nf - 1)
        def _():
            sweep(False)

        @pl.when(j == nf - 1)
        def _():
            sweep(True)

            def done(gidx, c):
                for u in range(MOE_DMA_UNROLL):
                    _moe_row_out(acc, y_hbm, osem, 0, 0).wait()
                return c

            lax.fori_loop(0, nsub * (MOE_SUB // MOE_DMA_UNROLL), done, 0)

    @pl.when((s == n_grid - 1) & (j == nf - 1))
    def _():
        tail0 = rs_ref[n_grid]
        n_blk = y_hbm.shape[0] // MOE_SUB
        acc[0:MOE_SUB, :] = jnp.zeros((MOE_SUB, acc.shape[1]), F32)

        def put(b, c):
            _moe_blk_out(acc, y_hbm, osem, b).start()
            return c

        lax.fori_loop(tail0, n_blk, put, 0)

        def done(b, c):
            _moe_blk_out(acc, y_hbm, osem, b).wait()
            return c

        lax.fori_loop(tail0, n_blk, done, 0)


def _moe(sup_e, sup_blk, sup_nsub, row_asg, h2p, w_gu, b_gu, w_dn, b_dn, n_rows):
    T = h2p.shape[0] // ROW_TILE
    D = 2 * ROW_TILE * LANES
    assert w_gu.shape[1] == D
    E, _, F2 = w_gu.shape
    F = F2 // 2
    tf = MOE_TF
    nf = F // tf
    assert MOE_SUB % nf == 0
    n_sup = sup_e.shape[0]

    def wmap(col0):
        def f(s, j, se, rs, ns, tok):
            return (se[s], 0, col0 + jnp.where(ns[s] > 0, j, nf - 1))
        return f

    def dmap(s, j, se, rs, ns, tok):
        return (se[s], jnp.where(ns[s] > 0, j, nf - 1), 0)

    def bdmap(s, j, se, rs, ns, tok):
        return (se[s], 0, 0)

    grid_spec = pltpu.PrefetchScalarGridSpec(
        num_scalar_prefetch=4,
        grid=(n_sup, nf),
        in_specs=[
            pl.BlockSpec(memory_space=pl.ANY),
            pl.BlockSpec((1, D, tf), wmap(0)),
            pl.BlockSpec((1, D, tf), wmap(nf)),
            pl.BlockSpec((1, tf, D), dmap),
            pl.BlockSpec((1, 1, tf), wmap(0)),
            pl.BlockSpec((1, 1, tf), wmap(nf)),
            pl.BlockSpec((1, 1, D), bdmap),
        ],
        out_specs=pl.BlockSpec(memory_space=pl.ANY),
        scratch_shapes=[
            pltpu.VMEM((2, MOE_SUP * ROW_TILE, LANES), U32),
            pltpu.VMEM((MOE_SUP, D), F32),
            pltpu.SemaphoreType.DMA((2,)),
            pltpu.SemaphoreType.DMA,
        ],
    )
    return pl.pallas_call(
        functools.partial(_moe_kernel, slot_bits=_slot_bits(n_rows), nf=nf),
        grid_spec=grid_spec,
        out_shape=jax.ShapeDtypeStruct((n_rows, D), F32),
        compiler_params=pltpu.CompilerParams(
            dimension_semantics=("arbitrary", "arbitrary"), vmem_limit_bytes=VMEM_LIMIT),
        name="moe",
    )(sup_e, sup_blk, sup_nsub, row_asg, h2p, w_gu, w_gu, w_dn,
      b_gu.reshape(E, 1, F2), b_gu.reshape(E, 1, F2), b_dn.reshape(E, 1, D))


def _combine_kernel(*refs):
    y_refs = refs[:TOP_K]
    x1_ref, gate_ref, nw_ref, o_ref = refs[TOP_K:]
    gates = gate_ref[...]
    x2 = x1_ref[...]
    for kk in range(TOP_K):
        x2 = x2 + gates[:, kk:kk + 1] * y_refs[kk][...]
    o_ref[...] = _rms(x2, nw_ref[...])


def _combine(y_slots, x1, gates, norm_w):
    T, D = x1.shape
    tm = COMB_TM
    tiles = T // tm

    def plane(kk):
        return pl.BlockSpec((tm, D), lambda i: (kk * tiles + i, 0))

    return pl.pallas_call(
        _combine_kernel,
        grid=(tiles,),
        in_specs=[plane(kk) for kk in range(TOP_K)] + [
            pl.BlockSpec((tm, D), lambda i: (i, 0)),
            pl.BlockSpec((tm, LANES), lambda i: (i, 0)),
            pl.BlockSpec((1, D), lambda i: (0, 0)),
        ],
        out_specs=pl.BlockSpec((tm, D), lambda i: (i, 0)),
        out_shape=jax.ShapeDtypeStruct((T, D), F32),
        compiler_params=pltpu.CompilerParams(
            dimension_semantics=("parallel",), vmem_limit_bytes=VMEM_LIMIT),
        name="combine",
    )(*([y_slots] * TOP_K), x1, gates, norm_w)


def _route(top_idx_t, n_experts):
    T = top_idx_t.shape[1]
    A = T * TOP_K
    G, SUP = MOE_SUB, MOE_SUP
    flat_e = top_idx_t.reshape(A)
    onehot = (flat_e[:, None] == jnp.arange(n_experts, dtype=I32)[None, :]).astype(I32)
    csum = jnp.cumsum(onehot, axis=0)
    counts = csum[-1]
    rank = jnp.sum((csum - 1) * onehot, axis=1)
    pcounts = (counts + G - 1) // G * G
    pends = jnp.cumsum(pcounts)
    pstarts = pends - pcounts
    dest = (pstarts[flat_e] + rank).astype(I32)
    n_rows = (A + n_experts * (G - 1)) // G * G
    p_ids = jnp.arange(n_rows, dtype=I32)
    seg = jnp.minimum(jnp.sum((p_ids[:, None] >= pends[None, :]).astype(I32), axis=1), n_experts - 1)
    pad_ids = A + p_ids - jnp.cumsum(counts)[seg]
    a_ids = jnp.arange(A, dtype=I32)
    bits = _slot_bits(n_rows)
    assert (T - 1) << bits < 2 ** 31
    row_asg = ((T - 1) << bits | pad_ids).astype(I32).at[dest].set((a_ids % T) << bits | a_ids)
    row_asg = jnp.concatenate([row_asg, jnp.zeros((SUP,), I32)])

    n_sup = (A + n_experts * (G - 1) + n_experts * (SUP - G)) // SUP + 1
    sub_e = pcounts // G
    nsup_e = (pcounts + SUP - 1) // SUP
    sup_ends = jnp.cumsum(nsup_e)
    total = sup_ends[-1]
    s_ids = jnp.arange(n_sup, dtype=I32)
    s_eff = jnp.minimum(s_ids, total - 1)
    e_of = jnp.sum((s_eff[:, None] >= sup_ends[None, :]).astype(I32), axis=1)
    local = s_eff - (sup_ends - nsup_e)[e_of]
    active = s_ids < total
    n_e = jnp.maximum(nsup_e[e_of], 1)
    small, extra = sub_e[e_of] // n_e, sub_e[e_of] % n_e
    blk0 = jnp.where(active, pstarts[e_of] // G + local * small + jnp.minimum(local, extra), 0)
    nsub = jnp.where(active, small + (local < extra).astype(I32), 0)
    order = jnp.argsort(-nsub, stable=True)
    e_sorted = e_of[order]
    sup_e = jnp.where(active, e_sorted, e_sorted[total - 1]).astype(I32)
    sup_blk = jnp.concatenate([blk0[order], pends[-1:] // G]).astype(I32)
    sup_nsub = nsub[order].astype(I32)
    return row_asg, n_rows, sup_e, sup_blk, sup_nsub


def _rope_tables(seq):
    half = HEAD_DIM // 2
    inv_freq = 1.0 / (ROPE_THETA ** (jnp.arange(0, HEAD_DIM, 2, dtype=F32) / HEAD_DIM))
    ang = jnp.arange(seq, dtype=F32)[:, None] * inv_freq[None, :]
    cos, sin = jnp.cos(ang), jnp.sin(ang)
    assert cos.shape[1] == half
    return jnp.concatenate([cos, cos], axis=1), jnp.concatenate([-sin, sin], axis=1)


def kernel(x, attn_norm_w, w_in, gla_gate_w2, gla_gate_b, gla_norm_w, w_out, ffn_norm_w,
           router_w, router_b, w_gate_up, b_gate_up, w_down, b_down, final_norm_w):
    B, S, D = x.shape
    T = B * S
    assert w_in.shape[0] == 1, "single-layer stack: the final norm is fused into combine"
    n_experts = router_w.shape[-1]
    n_main = w_in.shape[-1] - GLA_GATE_RANK
    cosf, sinf = _rope_tables(S)
    x2d = x.reshape(T, D)
    w_main = w_in[0, :, :n_main].astype(BF16)
    w_gz = jnp.pad(w_in[0, :, n_main:], ((0, 0), (0, LANES - GLA_GATE_RANK))).astype(BF16)
    w2_pad = jnp.pad(gla_gate_w2[0], ((0, LANES - GLA_GATE_RANK), (0, 0)))
    proj, log_a = _in_proj(x2d, attn_norm_w[0][None, :], w_main, w_gz, w2_pad,
                           gla_gate_b[0][None, :], cosf, sinf, S)
    moba_lo, moba_hi = _moba(proj, B, S)
    gla_o = _gla(proj, log_a, gla_norm_w[0][None, :], B, S)
    rw_t = jnp.pad(router_w[0].T, ((0, LANES - n_experts), (0, 0)))
    rw_hi = rw_t.astype(BF16)
    rw_lo = (rw_t - rw_hi.astype(F32)).astype(BF16)
    rb_col = jnp.pad(router_b[0], (0, LANES - n_experts), constant_values=NEG)
    rb_tile = jnp.broadcast_to(rb_col[:, None], (LANES, LANES))
    x1, h2p, top_idx_t, gates = _out_proj(moba_lo, moba_hi, gla_o, x2d, w_out[0].astype(BF16),
                                          ffn_norm_w[0][None, :], rw_hi, rw_lo, rb_tile, S)
    row_asg, n_rows, sup_e, sup_blk, sup_nsub = _route(top_idx_t[:TOP_K], n_experts)
    y_slots = _moe(sup_e, sup_blk, sup_nsub, row_asg, h2p, w_gate_up[0], b_gate_up[0],
                   w_down[0], b_down[0], n_rows)
    out = _combine(y_slots, x1, gates, final_norm_w[None, :])
    return out.reshape(B, S, D)
```

```python
import functools

import jax
import jax.numpy as jnp
from jax import lax
from jax.experimental import pallas as pl
from jax.experimental.pallas import tpu as pltpu

F32 = jnp.float32
BF16 = jnp.bfloat16
I32 = jnp.int32
U32 = jnp.uint32

HEAD_DIM = 128
MOBA_HEADS = 8
MOBA_BLOCK = 256
MOBA_TOPK = 3
ROPE_THETA = 10000.0
GLA_HEADS = 4
GLA_DK = 128
GLA_DV = 256
GLA_GATE_RANK = 16
GLA_GATE_NORMALIZER = 16.0
TOP_K = 4
SWIGLU_ALPHA = 1.702
SWIGLU_LIMIT = 7.0
NORM_EPS = 1e-5

LANES = 128
VMEM_LIMIT = 58 * 1024 * 1024
NEG = -1e30
LOG2_E = 1.4426950408889634

IN_PROJ_TM = 1024
PROJ_TM = 512
PROJ_TN = 1024
GLA_CHUNK = 64
GLA_GROUP = 16
GLA_HEADS_PER_STEP = 2
MOE_SUB = 256
MOE_SUP = 2048
MOE_TF = 256
MOE_DMA_UNROLL = 8
COMB_TM = 256


def _dot(a, b, **kw):
    return jnp.dot(a, b, preferred_element_type=F32, **kw)


def _dot_nt(a, b, **kw):
    return lax.dot_general(a, b, (((1,), (1,)), ((), ())), preferred_element_type=F32, **kw)


def _dot_tn(a, b, **kw):
    return lax.dot_general(a, b, (((0,), (0,)), ((), ())), preferred_element_type=F32, **kw)


def _rms(x, w):
    return x * lax.rsqrt(jnp.mean(x * x, axis=-1, keepdims=True) + NORM_EPS) * w


def _in_proj_kernel(x_ref, nw_ref, w_ref, wz_ref, w2_ref, gb_ref, cos_ref, sin_ref,
                    proj_ref, la_ref, h_scr, *, n_main, n_rope):
    j = pl.program_id(1)

    @pl.when(j == 0)
    def _():
        h_scr[...] = _rms(x_ref[...], nw_ref[...]).astype(BF16)

    @pl.when(j < n_rope)
    def _():
        acc = _dot(h_scr[...], w_ref[...])
        cosf = cos_ref[...]
        sinf = sin_ref[...]
        for hd in range(acc.shape[1] // HEAD_DIM):
            t = acc[:, hd * HEAD_DIM:(hd + 1) * HEAD_DIM]
            r = t * cosf + pltpu.roll(t, HEAD_DIM // 2, axis=1) * sinf
            proj_ref[:, hd * HEAD_DIM:(hd + 1) * HEAD_DIM] = r.astype(BF16)

    @pl.when((j >= n_rope) & (j < n_main))
    def _():
        proj_ref[...] = _dot(h_scr[...], w_ref[...]).astype(BF16)

    @pl.when(j == n_main)
    def _():
        gz = _dot(h_scr[...], wz_ref[...])
        gz_hi, gz_lo = _split_bf16(gz)
        w2_hi, w2_lo = _split_bf16(w2_ref[...])
        logits = _dot(gz_hi, w2_hi) + (_dot(gz_hi, w2_lo) + _dot(gz_lo, w2_hi)) + gb_ref[...]
        logsig = jnp.minimum(logits, 0.0) - jnp.log1p(jnp.exp(-jnp.abs(logits)))
        la_ref[...] = logsig * (1.0 / GLA_GATE_NORMALIZER)


def _in_proj(x2d, norm_w, w_main, w_gz, w2_pad, gate_b, cosf, sinf, seq):
    T, D = x2d.shape
    n_cols = w_main.shape[1]
    tm, tn = IN_PROJ_TM, PROJ_TN
    n_main = n_cols // tn
    n_rope = (2 * MOBA_HEADS * HEAD_DIM) // tn
    kw = w2_pad.shape[1]
    s_tiles = seq // tm
    last = n_main - 1
    return pl.pallas_call(
        functools.partial(_in_proj_kernel, n_main=n_main, n_rope=n_rope),
        grid=(T // tm, n_main + 1),
        in_specs=[
            pl.BlockSpec((tm, D), lambda i, j: (i, 0)),
            pl.BlockSpec((1, D), lambda i, j: (0, 0)),
            pl.BlockSpec((D, tn), lambda i, j: (0, jnp.minimum(j, last))),
            pl.BlockSpec((D, LANES), lambda i, j: (0, 0)),
            pl.BlockSpec((LANES, kw), lambda i, j: (0, 0)),
            pl.BlockSpec((1, kw), lambda i, j: (0, 0)),
            pl.BlockSpec((tm, HEAD_DIM), lambda i, j: (i % s_tiles, 0)),
            pl.BlockSpec((tm, HEAD_DIM), lambda i, j: (i % s_tiles, 0)),
        ],
        out_specs=[
            pl.BlockSpec((tm, tn), lambda i, j: (i, jnp.minimum(j, last))),
            pl.BlockSpec((tm, kw), lambda i, j: (i, 0)),
        ],
        out_shape=[
            jax.ShapeDtypeStruct((T, n_cols), BF16),
            jax.ShapeDtypeStruct((T, kw), F32),
        ],
        scratch_shapes=[pltpu.VMEM((tm, D), BF16)],
        compiler_params=pltpu.CompilerParams(
            dimension_semantics=("parallel", "arbitrary"), vmem_limit_bytes=VMEM_LIMIT),
        name="in_proj",
    )(x2d, norm_w, w_main, w_gz, w2_pad, gate_b, cosf, sinf)


MOBA_ONES_ROWS = 16


def _moba_select(q_ref, kmean_scr, qs_scr, which, iq, *, nb, blk, scale):
    hd = HEAD_DIM
    qt = q_ref[...].astype(F32).T
    qt_b = qt.astype(BF16)
    km = kmean_scr[...]
    km_hi, km_r = km.astype(BF16), km - km.astype(BF16).astype(F32)
    km_mid, km_lo = _split_bf16(km_r)
    gate = _dot(km_hi, qt_b) + (_dot(km_mid, qt_b) + _dot(km_lo, qt_b))
    n_iota = lax.broadcasted_iota(I32, (nb, blk), 0)
    rank = jnp.zeros((nb, blk), I32)
    for m in range(nb):
        gm = gate[m:m + 1, :]
        beats = (gm > gate) | ((gm == gate) & (m < n_iota))
        rank = rank + jnp.where(beats & (m < iq), 1, 0)
    keep = ((n_iota < iq) & (rank < MOBA_TOPK)) | (n_iota == iq)
    maskv = jnp.where(keep, 0.0, NEG)
    maskv = jnp.concatenate([maskv, jnp.zeros((hd - nb, blk), F32)], axis=0)
    qs_scr[which, :hd, :] = (qt * scale).astype(BF16)
    qs_scr[which, hd:, :] = maskv.astype(BF16)


def _moba_kernel(qa_ref, qb_ref, k_ref, v_ref, oa_ref, ob_ref,
                 kmean_scr, kaug_scr, vt_scr, qs_scr, s_scr, acc_scr, *, nb, blk, scale):
    p = pl.program_id(2)
    hd = HEAD_DIM
    sub = 8

    @pl.when(p == 0)
    def _():
        col = lax.broadcasted_iota(I32, (blk, hd), 1)
        ones = jnp.ones((MOBA_ONES_ROWS, blk), BF16)
        for n in range(nb):
            rows = slice(n * blk, (n + 1) * blk)
            kb = k_ref[rows, :]
            kmean_scr[n:n + 1, :] = jnp.sum(kb.astype(F32), axis=0, keepdims=True) * (1.0 / blk)
            kaug_scr[rows, :hd] = kb
            kaug_scr[rows, hd:] = jnp.where(col == n, 1.0, 0.0).astype(BF16)
            vt_scr[n, :hd, :] = v_ref[rows, :].astype(F32).T.astype(BF16)
            vt_scr[n, hd:, :] = ones

    ia = p
    ib = nb - 1 - p
    _moba_select(qa_ref, kmean_scr, qs_scr, 0, ia, nb=nb, blk=blk, scale=scale)
    _moba_select(qb_ref, kmean_scr, qs_scr, 1, ib, nb=nb, blk=blk, scale=scale)

    def fold_max(x):
        return jnp.max(x.reshape(blk // sub, sub, blk), axis=0)

    def key_rows(jb):
        return kaug_scr[pl.ds(pl.multiple_of(jb * blk, blk), blk), :]

    kpos = lax.broadcasted_iota(I32, (blk, blk), 0)
    qpos = lax.broadcasted_iota(I32, (blk, blk), 1)
    causal = jnp.where(kpos <= qpos, 0.0, NEG)

    s_own_a = _dot(key_rows(ia), qs_scr[0]) + causal
    s_own_b = _dot(key_rows(ib), qs_scr[1]) + causal
    mx_a = fold_max(s_own_a)
    mx_b = fold_max(s_own_b)
    trips = []
    for t in range(nb - 1):
        is_a = t < p
        which = jnp.where(is_a, 0, 1)
        jb = jnp.where(is_a, t, t - p)
        trips.append((is_a, which, jb))
        s = _dot(key_rows(jb), qs_scr[which])
        s_scr[t] = s
        f = fold_max(s)
        mx_a = jnp.maximum(mx_a, jnp.where(is_a, f, NEG))
        mx_b = jnp.maximum(mx_b, jnp.where(is_a, NEG, f))
    m_a = jnp.max(mx_a, axis=0, keepdims=True)
    m_b = jnp.max(mx_b, axis=0, keepdims=True)

    acc_scr[0] = _dot(vt_scr[ia], jnp.exp2(s_own_a - m_a).astype(BF16))
    acc_scr[1] = _dot(vt_scr[ib], jnp.exp2(s_own_b - m_b).astype(BF16))
    for t, (is_a, which, jb) in enumerate(trips):
        pr = jnp.exp2(s_scr[t] - jnp.where(is_a, m_a, m_b))
        da = _dot(vt_scr[jb], pr.astype(BF16))
        acc_scr[0] += jnp.where(is_a, da, 0.0)
        acc_scr[1] += jnp.where(is_a, 0.0, da)
    for which, o_ref in ((0, oa_ref), (1, ob_ref)):
        acc = acc_scr[which]
        o_ref[...] = (acc[:hd] / acc[hd:hd + 1]).T.astype(BF16)


def _moba(proj, batch, seq):
    blk = MOBA_BLOCK
    nb = seq // blk
    assert nb % 2 == 0 and nb <= HEAD_DIM
    half = nb // 2
    H = MOBA_HEADS
    hd = HEAD_DIM
    width = H * hd
    lo, hi = pl.pallas_call(
        functools.partial(_moba_kernel, nb=nb, blk=blk, scale=hd ** -0.5 * LOG2_E),
        grid=(batch, H, half),
        in_specs=[
            pl.BlockSpec((blk, hd), lambda b, h, p: (b * nb + p, h)),
            pl.BlockSpec((blk, hd), lambda b, h, p: (b * nb + nb - 1 - p, h)),
            pl.BlockSpec((seq, hd), lambda b, h, p: (b, H + h)),
            pl.BlockSpec((seq, hd), lambda b, h, p: (b, 2 * H + h)),
        ],
        out_specs=[
            pl.BlockSpec((blk, hd), lambda b, h, p: (b * half + p, h)),
            pl.BlockSpec((blk, hd), lambda b, h, p: (b * half + half - 1 - p, h)),
        ],
        out_shape=[jax.ShapeDtypeStruct((batch * half * blk, width), BF16)] * 2,
        scratch_shapes=[
            pltpu.VMEM((nb, hd), F32),
            pltpu.VMEM((seq, 2 * hd), BF16),
            pltpu.VMEM((nb, hd + MOBA_ONES_ROWS, blk), BF16),
            pltpu.VMEM((2, 2 * hd, blk), BF16),
            pltpu.VMEM((nb - 1, blk, blk), F32),
            pltpu.VMEM((2, hd + MOBA_ONES_ROWS, blk), F32),
        ],
        compiler_params=pltpu.CompilerParams(
            dimension_semantics=("parallel", "parallel", "arbitrary"), vmem_limit_bytes=VMEM_LIMIT),
        name="moba",
    )(proj, proj, proj, proj)
    return lo, hi


def _gla_levels(c):
    out = []
    m = c // 2
    while m >= 1:
        out.append(m)
        m //= 2
    return out


def _gla_kernel(q_ref, k_ref, v_ref, g_ref, la_ref, nw_ref, o_ref, st_scr, *,
                chunk, group, heads, scale):
    C = chunk
    levels = _gla_levels(C)

    @pl.when(pl.program_id(2) == 0)
    def _():
        st_scr[...] = jnp.zeros_like(st_scr)

    t_io = lax.broadcasted_iota(I32, (C, C), 0)
    s_io = lax.broadcasted_iota(I32, (C, C), 1)
    sel_rows = [jnp.where(s_io <= t_io, 1.0, 0.0)]
    masks = []
    for m in levels:
        blk_t = t_io // (2 * m)
        bound = blk_t * (2 * m) + (m - 1)
        sel_rows.append(jnp.where(s_io <= bound, 1.0, 0.0))
        masks.append((blk_t == s_io // (2 * m)) & (t_io % (2 * m) >= m) & (s_io % (2 * m) < m))
    cum_mat = jnp.concatenate(sel_rows, axis=0).astype(BF16)
    diag = t_io == s_io
    nw = nw_ref[...]

    dk, dv = GLA_DK, GLA_DV
    for c in range(group):
        rows = slice(c * C, (c + 1) * C)
        for hh in range(heads):
            kc = slice(hh * dk, (hh + 1) * dk)
            vc = slice(hh * dv, (hh + 1) * dv)
            la_hi, la_lo = _split_bf16(la_ref[rows, kc])
            cums = _dot(cum_mat, la_hi) + _dot(cum_mat, la_lo)
            b = cums[:C]
            q = q_ref[rows, kc].astype(F32) * scale
            k = k_ref[rows, kc].astype(F32)
            v = v_ref[rows, vc]
            a = jnp.where(diag, _dot_nt(q.astype(BF16), k.astype(BF16)), 0.0)
            for li in range(len(levels)):
                e = jnp.exp(-jnp.abs(b - cums[(li + 1) * C:(li + 2) * C]))
                a = a + jnp.where(masks[li], _dot_nt((q * e).astype(BF16), (k * e).astype(BF16)), 0.0)
            st = st_scr[hh]
            o = _dot(a.astype(BF16), v) + _dot_nt((q * jnp.exp(b)).astype(BF16), st.astype(BF16))
            b_last = b[C - 1:C, :]
            ke = (k * jnp.exp(b_last - b)).astype(BF16)
            st_scr[hh] = st * jnp.exp(b_last) + _dot_tn(v, ke)
            gg = g_ref[rows, vc].astype(F32)
            o_ref[rows, vc] = (_rms(o, nw) * (gg * jax.nn.sigmoid(gg))).astype(BF16)


def _gla(proj, la, norm_w, batch, seq):
    T = proj.shape[0]
    Hg = GLA_HEADS
    rows = GLA_CHUNK * GLA_GROUP
    ng = seq // rows
    hs = GLA_HEADS_PER_STEP
    wk, wv = hs * GLA_DK, hs * GLA_DV
    q0 = (3 * MOBA_HEADS * HEAD_DIM) // wk
    k0 = q0 + Hg // hs
    v0 = (3 * MOBA_HEADS * HEAD_DIM + 2 * Hg * GLA_DK) // wv
    g0 = v0 + Hg // hs
    return pl.pallas_call(
        functools.partial(_gla_kernel, chunk=GLA_CHUNK, group=GLA_GROUP, heads=hs,
                          scale=GLA_DK ** -0.5),
        grid=(batch, Hg // hs, ng),
        in_specs=[
            pl.BlockSpec((rows, wk), lambda b, h, g: (b * ng + g, q0 + h)),
            pl.BlockSpec((rows, wk), lambda b, h, g: (b * ng + g, k0 + h)),
            pl.BlockSpec((rows, wv), lambda b, h, g: (b * ng + g, v0 + h)),
            pl.BlockSpec((rows, wv), lambda b, h, g: (b * ng + g, g0 + h)),
            pl.BlockSpec((rows, wk), lambda b, h, g: (b * ng + g, h)),
            pl.BlockSpec((1, GLA_DV), lambda b, h, g: (0, 0)),
        ],
        out_specs=pl.BlockSpec((rows, wv), lambda b, h, g: (b * ng + g, h)),
        out_shape=jax.ShapeDtypeStruct((T, Hg * GLA_DV), BF16),
        scratch_shapes=[pltpu.VMEM((hs, GLA_DV, GLA_DK), F32)],
        compiler_params=pltpu.CompilerParams(
            dimension_semantics=("parallel", "parallel", "arbitrary"), vmem_limit_bytes=VMEM_LIMIT),
        name="gla",
    )(proj, proj, proj, proj, la, norm_w)


def _split_bf16(x):
    hi = x.astype(BF16)
    return hi, (x - hi.astype(F32)).astype(BF16)


def _pack_bf16_pairs(x):
    n = x.shape[1] // 2
    lo = lax.bitcast_convert_type(x[:, :n].astype(BF16).astype(F32), U32)
    hi = lax.bitcast_convert_type(x[:, n:].astype(BF16).astype(F32), U32)
    return (lo >> 16) | (hi & jnp.uint32(0xFFFF0000))


def _unpack_bf16_pairs(p):
    lo = lax.bitcast_convert_type(p << 16, F32).astype(BF16)
    hi = lax.bitcast_convert_type(p & jnp.uint32(0xFFFF0000), F32).astype(BF16)
    return lo, hi


ROW_TILE = 8


def _token_tile(ref, t):
    return ref.at[pl.ds(pl.multiple_of(t * ROW_TILE, ROW_TILE), ROW_TILE), :]


def _store_token_rows(ref, x):
    n = x.shape[0]
    for c in range(ROW_TILE):
        ref[pl.ds(c, n, stride=ROW_TILE), :] = x[:, c * LANES:(c + 1) * LANES]


def _load_token_rows(ref, t0, n):
    base = pl.multiple_of(t0 * ROW_TILE, ROW_TILE)
    return jnp.concatenate(
        [ref[pl.ds(base + c, n, stride=ROW_TILE), :] for c in range(ROW_TILE)], axis=1)


def _out_proj_kernel(mlo_ref, mhi_ref, gl_ref, x_ref, wo_ref, nw_ref, rwh_ref, rwl_ref, rb_ref,
                     x1_ref, h2p_ref, idx_ref, gate_ref, rank_ref, cnt_ref, mo_scr, cnt_scr, *, seq_tiles):
    wm = mlo_ref.shape[1]
    first_half = (pl.program_id(0) % seq_tiles) < seq_tiles // 2
    @pl.when(first_half)
    def _():
        mo_scr[...] = mlo_ref[...]

    @pl.when(jnp.logical_not(first_half))
    def _():
        mo_scr[...] = mhi_ref[...]

    x1 = x_ref[...] + _dot(mo_scr[...], wo_ref[:wm, :]) + _dot(gl_ref[...], wo_ref[wm:, :])
    x1_ref[...] = x1
    h2 = _rms(x1, nw_ref[...])
    _store_token_rows(h2p_ref, _pack_bf16_pairs(h2))
    h_hi, h_lo = _split_bf16(h2)
    rw_hi = rwh_ref[...]
    logits = _dot_nt(rw_hi, h_hi) + (_dot_nt(rwl_ref[...], h_hi) + _dot_nt(rw_hi, h_lo))
    tm = logits.shape[1]
    logits = logits + jnp.concatenate([rb_ref[...]] * (tm // LANES), axis=1)
    row = lax.broadcasted_iota(I32, logits.shape, 0)
    vals, idxs = [], []
    for _ in range(TOP_K):
        mx = jnp.max(logits, axis=0, keepdims=True)
        ix = jnp.min(jnp.where(logits == mx, row, LANES), axis=0, keepdims=True)
        vals.append(mx)
        idxs.append(ix)
        logits = jnp.where(row == ix, -jnp.inf, logits)
    exps = [jnp.exp(v - vals[0]) for v in vals]
    denom = exps[0]
    for e in exps[1:]:
        denom = denom + e
    idx_t = jnp.zeros(logits.shape, I32)
    gate_t = jnp.zeros(logits.shape, F32)
    for r in range(TOP_K):
        idx_t = jnp.where(row == r, idxs[r], idx_t)
        gate_t = jnp.where(row == r, exps[r] / denom, gate_t)
    idx_ref[...] = idx_t[:idx_ref.shape[0], :]
    gate_ref[...] = gate_t.T

    @pl.when(pl.program_id(0) == 0)
    def _():
        cnt_scr[...] = jnp.zeros_like(cnt_scr)

    picks = [row == idxs[r] for r in range(TOP_K)]
    picked = picks[0]
    for pk in picks[1:]:
        picked = picked | pk
    cmat = jnp.where(picked, 1.0, 0.0).astype(BF16)
    t_r = lax.broadcasted_iota(I32, (tm, tm), 0)
    t_c = lax.broadcasted_iota(I32, (tm, tm), 1)
    earlier = jnp.where(t_r < t_c, 1.0, 0.0).astype(BF16)
    base = cnt_scr[...]
    before = _dot(cmat, earlier) + jnp.concatenate([base] * (tm // LANES), axis=1)
    rank_t = jnp.zeros(logits.shape, F32)
    for r in range(TOP_K):
        rank_t = jnp.where(row == r, jnp.sum(jnp.where(picks[r], before, 0.0), axis=0, keepdims=True), rank_t)
    rank_ref[...] = rank_t[:rank_ref.shape[0], :].astype(I32)
    cnt_scr[...] = base + _dot(cmat, jnp.ones((tm, LANES), BF16))
    cnt_ref[...] = cnt_scr[...]


def _out_proj(moba_lo, moba_hi, gla_o, x2d, w_out, norm_w, rw_hi, rw_lo, rb_tile, seq):
    T, D = x2d.shape
    tm = PROJ_TM
    wm, wg = moba_lo.shape[1], gla_o.shape[1]
    st = seq // tm
    ht = st // 2
    assert st % 2 == 0

    def lo_map(i):
        return ((i // st) * ht + jnp.minimum(i % st, ht - 1), 0)

    def hi_map(i):
        return ((i // st) * ht + jnp.maximum(i % st - ht, 0), 0)

    return pl.pallas_call(
        functools.partial(_out_proj_kernel, seq_tiles=st),
        grid=(T // tm,),
        in_specs=[
            pl.BlockSpec((tm, wm), lo_map),
            pl.BlockSpec((tm, wm), hi_map),
            pl.BlockSpec((tm, wg), lambda i: (i, 0)),
            pl.BlockSpec((tm, D), lambda i: (i, 0)),
            pl.BlockSpec((wm + wg, D), lambda i: (0, 0)),
            pl.BlockSpec((1, D), lambda i: (0, 0)),
            pl.BlockSpec((LANES, D), lambda i: (0, 0)),
            pl.BlockSpec((LANES, D), lambda i: (0, 0)),
            pl.BlockSpec((LANES, LANES), lambda i: (0, 0)),
        ],
        out_specs=[
            pl.BlockSpec((tm, D), lambda i: (i, 0)),
            pl.BlockSpec((tm * ROW_TILE, LANES), lambda i: (i, 0)),
            pl.BlockSpec((ROW_TILE, tm), lambda i: (0, i)),
            pl.BlockSpec((tm, LANES), lambda i: (i, 0)),
            pl.BlockSpec((ROW_TILE, tm), lambda i: (0, i)),
            pl.BlockSpec((LANES, LANES), lambda i: (0, 0)),
        ],
        out_shape=[
            jax.ShapeDtypeStruct((T, D), F32),
            jax.ShapeDtypeStruct((T * ROW_TILE, LANES), U32),
            jax.ShapeDtypeStruct((ROW_TILE, T), I32),
            jax.ShapeDtypeStruct((T, LANES), F32),
            jax.ShapeDtypeStruct((ROW_TILE, T), I32),
            jax.ShapeDtypeStruct((LANES, LANES), F32),
        ],
        scratch_shapes=[pltpu.VMEM((tm, wm), BF16), pltpu.VMEM((LANES, LANES), F32)],
        compiler_params=pltpu.CompilerParams(
            dimension_semantics=("arbitrary",), vmem_limit_bytes=VMEM_LIMIT),
        name="out_proj",
    )(moba_lo, moba_hi, gla_o, x2d, w_out, norm_w, rw_hi, rw_lo, rb_tile)


def _slot_bits(n_rows):
    return (n_rows - 1).bit_length()


def _moe_row_in(h_hbm, xbuf, sem, slot, tok, r):
    return pltpu.make_async_copy(_token_tile(h_hbm, tok), _token_tile(xbuf.at[slot], r), sem.at[slot])


def _moe_row_out(acc, y_hbm, sem, r, dst_row):
    return pltpu.make_async_copy(acc.at[pl.ds(r, 1), :], y_hbm.at[pl.ds(dst_row, 1), :], sem)


def _moe_blk_out(acc, y_hbm, sem, blk):
    dst = y_hbm.at[pl.ds(pl.multiple_of(blk * MOE_SUB, MOE_SUB), MOE_SUB), :]
    return pltpu.make_async_copy(acc.at[pl.ds(0, MOE_SUB), :], dst, sem)


def _moe_kernel(se_ref, rs_ref, ns_ref, asg_ref,
                h_hbm, wg_ref, wu_ref, wd_ref, bg_ref, bu_ref, bd_ref, y_hbm,
                xbuf, acc, gsem, osem, *, slot_bits, nf):
    s = pl.program_id(0)
    j = pl.program_id(1)
    n_grid = pl.num_programs(0)
    nsub = ns_ref[s]
    base = rs_ref[s] * MOE_SUB
    slot = s & 1
    per_trip = MOE_SUB // nf
    half = wg_ref.shape[1] // 2

    def src_token(p):
        return lax.shift_right_logical(asg_ref[p], slot_bits)

    def out_slot(p):
        return asg_ref[p] & ((1 << slot_bits) - 1)

    def wait_rows(n_sub_tiles):
        def wait(gidx, c):
            for u in range(MOE_DMA_UNROLL):
                _moe_row_in(h_hbm, xbuf, gsem, slot, 0, gidx * MOE_DMA_UNROLL + u).wait()
            return c

        lax.fori_loop(0, n_sub_tiles * (MOE_SUB // MOE_DMA_UNROLL), wait, 0)

    @pl.when(j == 0)
    def _():
        @pl.when(s == 0)
        def _():
            def issue(gidx, c):
                for u in range(MOE_DMA_UNROLL):
                    r = gidx * MOE_DMA_UNROLL + u
                    _moe_row_in(h_hbm, xbuf, gsem, slot, src_token(base + r), r).start()
                return c

            lax.fori_loop(0, nsub * (MOE_SUB // MOE_DMA_UNROLL), issue, 0)

        wait_rows(jnp.where(s == 0, nsub, ns_ref[jnp.maximum(s - 1, 0)]))

    @pl.when(nsub > 0)
    def _():
        @pl.when(j == 0)
        def _():
            bias_rows = jnp.broadcast_to(bd_ref[0], (MOE_SUB, acc.shape[1]))

            def init(r, c):
                acc[pl.ds(pl.multiple_of(r * MOE_SUB, MOE_SUB), MOE_SUB), :] = bias_rows
                return c

            lax.fori_loop(0, nsub, init, 0)

        bg = bg_ref[0]
        bu = bu_ref[0]
        next_base = rs_ref[s + 1] * MOE_SUB

        def tile(r, n, emit):
            rows = n * MOE_SUB
            off = pl.multiple_of(r * MOE_SUB, MOE_SUB)
            xl, xh = _unpack_bf16_pairs(_load_token_rows(xbuf.at[slot], off, rows))
            g = (_dot(xl, wg_ref[0, :half, :].astype(BF16))
                 + _dot(xh, wg_ref[0, half:, :].astype(BF16)) + bg)
            u = (_dot(xl, wu_ref[0, :half, :].astype(BF16))
                 + _dot(xh, wu_ref[0, half:, :].astype(BF16)) + bu)
            g = jnp.minimum(g, SWIGLU_LIMIT)
            u = jnp.clip(u, -SWIGLU_LIMIT, SWIGLU_LIMIT)
            act = (u + 1.0) * (g * jax.nn.sigmoid(SWIGLU_ALPHA * g))
            acc[pl.ds(off, rows), :] += _dot(act.astype(BF16), wd_ref[0].astype(BF16))
            n_in = n * per_trip
            g0 = pl.multiple_of((j * nsub + r) * per_trip, per_trip)
            x_next = xbuf.at[1 - slot, pl.ds(g0 * ROW_TILE, n_in * ROW_TILE), :]
            for v in range(n_in):
                tok = src_token(next_base + g0 + v)
                pltpu.make_async_copy(_token_tile(h_hbm, tok), _token_tile(x_next, v),
                                      gsem.at[1 - slot]).start()
            if emit:
                acc_rows = acc.at[pl.ds(off, rows), :]
                for v in range(rows):
                    _moe_row_out(acc_rows, y_hbm, osem, v, out_slot(base + off + v)).start()

        def sweep(emit):
            def pair(t, c):
                tile(2 * t, 2, emit)
                return c

            lax.fori_loop(0, nsub >> 1, pair, 0)

            @pl.when((nsub & 1) == 1)
            def _():
                tile(nsub - 1, 1, emit)

        @pl.when(j < nf - 1)
        def _():
            sweep(False)

        @pl.when(j == nf - 1)
        def _():
            sweep(True)

            def done(gidx, c):
                for u in range(MOE_DMA_UNROLL):
                    _moe_row_out(acc, y_hbm, osem, 0, 0).wait()
                return c

            lax.fori_loop(0, nsub * (MOE_SUB // MOE_DMA_UNROLL), done, 0)

    @pl.when((s == n_grid - 1) & (j == nf - 1))
    def _():
        tail0 = rs_ref[n_grid]
        n_blk = y_hbm.shape[0] // MOE_SUB
        acc[0:MOE_SUB, :] = jnp.zeros((MOE_SUB, acc.shape[1]), F32)

        def put(b, c):
            _moe_blk_out(acc, y_hbm, osem, b).start()
            return c

        lax.fori_loop(tail0, n_blk, put, 0)

        def done(b, c):
            _moe_blk_out(acc, y_hbm, osem, b).wait()
            return c

        lax.fori_loop(tail0, n_blk, done, 0)


def _moe(sup_e, sup_blk, sup_nsub, row_asg, h2p, w_gu, b_gu, w_dn, b_dn, n_rows):
    T = h2p.shape[0] // ROW_TILE
    D = 2 * ROW_TILE * LANES
    assert w_gu.shape[1] == D
    E, _, F2 = w_gu.shape
    F = F2 // 2
    tf = MOE_TF
    nf = F // tf
    assert MOE_SUB % nf == 0
    n_sup = sup_e.shape[0]

    def wmap(col0):
        def f(s, j, se, rs, ns, tok):
            return (se[s], 0, col0 + jnp.where(ns[s] > 0, j, nf - 1))
        return f

    def dmap(s, j, se, rs, ns, tok):
        return (se[s], jnp.where(ns[s] > 0, j, nf - 1), 0)

    def bdmap(s, j, se, rs, ns, tok):
        return (se[s], 0, 0)

    grid_spec = pltpu.PrefetchScalarGridSpec(
        num_scalar_prefetch=4,
        grid=(n_sup, nf),
        in_specs=[
            pl.BlockSpec(memory_space=pl.ANY),
            pl.BlockSpec((1, D, tf), wmap(0)),
            pl.BlockSpec((1, D, tf), wmap(nf)),
            pl.BlockSpec((1, tf, D), dmap),
            pl.BlockSpec((1, 1, tf), wmap(0)),
            pl.BlockSpec((1, 1, tf), wmap(nf)),
            pl.BlockSpec((1, 1, D), bdmap),
        ],
        out_specs=pl.BlockSpec(memory_space=pl.ANY),
        scratch_shapes=[
            pltpu.VMEM((2, MOE_SUP * ROW_TILE, LANES), U32),
            pltpu.VMEM((MOE_SUP, D), F32),
            pltpu.SemaphoreType.DMA((2,)),
            pltpu.SemaphoreType.DMA,
        ],
    )
    return pl.pallas_call(
        functools.partial(_moe_kernel, slot_bits=_slot_bits(n_rows), nf=nf),
        grid_spec=grid_spec,
        out_shape=jax.ShapeDtypeStruct((n_rows, D), F32),
        compiler_params=pltpu.CompilerParams(
            dimension_semantics=("arbitrary", "arbitrary"), vmem_limit_bytes=VMEM_LIMIT),
        name="moe",
    )(sup_e, sup_blk, sup_nsub, row_asg, h2p, w_gu, w_gu, w_dn,
      b_gu.reshape(E, 1, F2), b_gu.reshape(E, 1, F2), b_dn.reshape(E, 1, D))


def _combine_kernel(*refs):
    y_refs = refs[:TOP_K]
    x1_ref, gate_ref, nw_ref, o_ref = refs[TOP_K:]
    gates = gate_ref[...]
    x2 = x1_ref[...]
    for kk in range(TOP_K):
        x2 = x2 + gates[:, kk:kk + 1] * y_refs[kk][...]
    o_ref[...] = _rms(x2, nw_ref[...])


def _combine(y_slots, x1, gates, norm_w):
    T, D = x1.shape
    tm = COMB_TM
    tiles = T // tm

    def plane(kk):
        return pl.BlockSpec((tm, D), lambda i: (kk * tiles + i, 0))

    return pl.pallas_call(
        _combine_kernel,
        grid=(tiles,),
        in_specs=[plane(kk) for kk in range(TOP_K)] + [
            pl.BlockSpec((tm, D), lambda i: (i, 0)),
            pl.BlockSpec((tm, LANES), lambda i: (i, 0)),
            pl.BlockSpec((1, D), lambda i: (0, 0)),
        ],
        out_specs=pl.BlockSpec((tm, D), lambda i: (i, 0)),
        out_shape=jax.ShapeDtypeStruct((T, D), F32),
        compiler_params=pltpu.CompilerParams(
            dimension_semantics=("parallel",), vmem_limit_bytes=VMEM_LIMIT),
        name="combine",
    )(*([y_slots] * TOP_K), x1, gates, norm_w)


def _route(top_idx_t, rank_t, counts, n_experts):
    T = top_idx_t.shape[1]
    A = T * TOP_K
    G, SUP = MOE_SUB, MOE_SUP
    flat_e = top_idx_t.reshape(A)
    rank = rank_t.reshape(A)
    pcounts = (counts + G - 1) // G * G
    pends = jnp.cumsum(pcounts)
    pstarts = pends - pcounts
    dest = (pstarts[flat_e] + rank).astype(I32)
    n_rows = (A + n_experts * (G - 1)) // G * G
    p_ids = jnp.arange(n_rows, dtype=I32)
    seg = jnp.minimum(jnp.sum((p_ids[:, None] >= pends[None, :]).astype(I32), axis=1), n_experts - 1)
    pad_ids = A + p_ids - jnp.cumsum(counts)[seg]
    a_ids = jnp.arange(A, dtype=I32)
    bits = _slot_bits(n_rows)
    assert (T - 1) << bits < 2 ** 31
    row_asg = ((T - 1) << bits | pad_ids).astype(I32).at[dest].set((a_ids % T) << bits | a_ids)
    row_asg = jnp.concatenate([row_asg, jnp.zeros((SUP,), I32)])

    n_sup = (A + n_experts * (G - 1) + n_experts * (SUP - G)) // SUP + 1
    sub_e = pcounts // G
    nsup_e = (pcounts + SUP - 1) // SUP
    sup_ends = jnp.cumsum(nsup_e)
    total = sup_ends[-1]
    s_ids = jnp.arange(n_sup, dtype=I32)
    s_eff = jnp.minimum(s_ids, total - 1)
    e_of = jnp.sum((s_eff[:, None] >= sup_ends[None, :]).astype(I32), axis=1)
    local = s_eff - (sup_ends - nsup_e)[e_of]
    active = s_ids < total
    n_e = jnp.maximum(nsup_e[e_of], 1)
    small, extra = sub_e[e_of] // n_e, sub_e[e_of] % n_e
    blk0 = jnp.where(active, pstarts[e_of] // G + local * small + jnp.minimum(local, extra), 0)
    nsub = jnp.where(active, small + (local < extra).astype(I32), 0)
    order = jnp.argsort(-nsub, stable=True)
    e_sorted = e_of[order]
    sup_e = jnp.where(active, e_sorted, e_sorted[total - 1]).astype(I32)
    sup_blk = jnp.concatenate([blk0[order], pends[-1:] // G]).astype(I32)
    sup_nsub = nsub[order].astype(I32)
    return row_asg, n_rows, sup_e, sup_blk, sup_nsub


def _rope_tables(seq):
    half = HEAD_DIM // 2
    inv_freq = 1.0 / (ROPE_THETA ** (jnp.arange(0, HEAD_DIM, 2, dtype=F32) / HEAD_DIM))
    ang = jnp.arange(seq, dtype=F32)[:, None] * inv_freq[None, :]
    cos, sin = jnp.cos(ang), jnp.sin(ang)
    assert cos.shape[1] == half
    return jnp.concatenate([cos, cos], axis=1), jnp.concatenate([-sin, sin], axis=1)


def kernel(x, attn_norm_w, w_in, gla_gate_w2, gla_gate_b, gla_norm_w, w_out, ffn_norm_w,
           router_w, router_b, w_gate_up, b_gate_up, w_down, b_down, final_norm_w):
    B, S, D = x.shape
    T = B * S
    assert w_in.shape[0] == 1, "single-layer stack: the final norm is fused into combine"
    n_experts = router_w.shape[-1]
    n_main = w_in.shape[-1] - GLA_GATE_RANK
    cosf, sinf = _rope_tables(S)
    x2d = x.reshape(T, D)
    w_main = w_in[0, :, :n_main].astype(BF16)
    w_gz = jnp.pad(w_in[0, :, n_main:], ((0, 0), (0, LANES - GLA_GATE_RANK))).astype(BF16)
    w2_pad = jnp.pad(gla_gate_w2[0], ((0, LANES - GLA_GATE_RANK), (0, 0)))
    proj, log_a = _in_proj(x2d, attn_norm_w[0][None, :], w_main, w_gz, w2_pad,
                           gla_gate_b[0][None, :], cosf, sinf, S)
    moba_lo, moba_hi = _moba(proj, B, S)
    gla_o = _gla(proj, log_a, gla_norm_w[0][None, :], B, S)
    rw_t = jnp.pad(router_w[0].T, ((0, LANES - n_experts), (0, 0)))
    rw_hi = rw_t.astype(BF16)
    rw_lo = (rw_t - rw_hi.astype(F32)).astype(BF16)
    rb_col = jnp.pad(router_b[0], (0, LANES - n_experts), constant_values=NEG)
    rb_tile = jnp.broadcast_to(rb_col[:, None], (LANES, LANES))
    x1, h2p, top_idx_t, gates, rank_t, counts = _out_proj(
        moba_lo, moba_hi, gla_o, x2d, w_out[0].astype(BF16), ffn_norm_w[0][None, :], rw_hi, rw_lo, rb_tile, S)
    row_asg, n_rows, sup_e, sup_blk, sup_nsub = _route(
        top_idx_t[:TOP_K], rank_t[:TOP_K], counts[:n_experts, 0].astype(I32), n_experts)
    y_slots = _moe(sup_e, sup_blk, sup_nsub, row_asg, h2p, w_gate_up[0], b_gate_up[0],
                   w_down[0], b_down[0], n_rows)
    out = _combine(y_slots, x1, gates, final_norm_w[None, :])
    return out.reshape(B, S, D)
```

```python
import functools

import jax
import jax.numpy as jnp
from jax import lax
from jax.experimental import pallas as pl
from jax.experimental.pallas import tpu as pltpu

F32 = jnp.float32
BF16 = jnp.bfloat16
I32 = jnp.int32
U32 = jnp.uint32

HEAD_DIM = 128
MOBA_HEADS = 8
MOBA_BLOCK = 256
MOBA_TOPK = 3
ROPE_THETA = 10000.0
GLA_HEADS = 4
GLA_DK = 128
GLA_DV = 256
GLA_GATE_RANK = 16
GLA_GATE_NORMALIZER = 16.0
TOP_K = 4
SWIGLU_ALPHA = 1.702
SWIGLU_LIMIT = 7.0
NORM_EPS = 1e-5

LANES = 128
VMEM_LIMIT = 58 * 1024 * 1024
NEG = -1e30
LOG2_E = 1.4426950408889634

IN_PROJ_TM = 1024
PROJ_TM = 512
PROJ_TN = 1024
GLA_CHUNK = 64
GLA_GROUP = 16
GLA_HEADS_PER_STEP = 2
MOE_SUB = 256
MOE_SUP = 2048
MOE_TF = 256
MOE_DMA_UNROLL = 8
COMB_TM = 256


def _dot(a, b, **kw):
    return jnp.dot(a, b, preferred_element_type=F32, **kw)


def _dot_nt(a, b, **kw):
    return lax.dot_general(a, b, (((1,), (1,)), ((), ())), preferred_element_type=F32, **kw)


def _dot_tn(a, b, **kw):
    return lax.dot_general(a, b, (((0,), (0,)), ((), ())), preferred_element_type=F32, **kw)


def _rms(x, w):
    return x * lax.rsqrt(jnp.mean(x * x, axis=-1, keepdims=True) + NORM_EPS) * w


def _in_proj_kernel(x_ref, nw_ref, w_ref, wz_ref, w2_ref, gb_ref, cos_ref, sin_ref,
                    proj_ref, la_ref, h_scr, *, n_main, n_rope):
    j = pl.program_id(1)

    @pl.when(j == 0)
    def _():
        h_scr[...] = _rms(x_ref[...], nw_ref[...]).astype(BF16)

    @pl.when(j < n_rope)
    def _():
        acc = _dot(h_scr[...], w_ref[...])
        cosf = cos_ref[...]
        sinf = sin_ref[...]
        for hd in range(acc.shape[1] // HEAD_DIM):
            t = acc[:, hd * HEAD_DIM:(hd + 1) * HEAD_DIM]
            r = t * cosf + pltpu.roll(t, HEAD_DIM // 2, axis=1) * sinf
            proj_ref[:, hd * HEAD_DIM:(hd + 1) * HEAD_DIM] = r.astype(BF16)

    @pl.when((j >= n_rope) & (j < n_main))
    def _():
        proj_ref[...] = _dot(h_scr[...], w_ref[...]).astype(BF16)

    @pl.when(j == n_main)
    def _():
        gz = _dot(h_scr[...], wz_ref[...])
        gz_hi, gz_lo = _split_bf16(gz)
        w2_hi, w2_lo = _split_bf16(w2_ref[...])
        logits = _dot(gz_hi, w2_hi) + (_dot(gz_hi, w2_lo) + _dot(gz_lo, w2_hi)) + gb_ref[...]
        logsig = jnp.minimum(logits, 0.0) - jnp.log1p(jnp.exp(-jnp.abs(logits)))
        la_ref[...] = logsig * (1.0 / GLA_GATE_NORMALIZER)


def _in_proj(x2d, norm_w, w_main, w_gz, w2_pad, gate_b, cosf, sinf, seq):
    T, D = x2d.shape
    n_cols = w_main.shape[1]
    tm, tn = IN_PROJ_TM, PROJ_TN
    n_main = n_cols // tn
    n_rope = (2 * MOBA_HEADS * HEAD_DIM) // tn
    kw = w2_pad.shape[1]
    s_tiles = seq // tm
    last = n_main - 1
    return pl.pallas_call(
        functools.partial(_in_proj_kernel, n_main=n_main, n_rope=n_rope),
        grid=(T // tm, n_main + 1),
        in_specs=[
            pl.BlockSpec((tm, D), lambda i, j: (i, 0)),
            pl.BlockSpec((1, D), lambda i, j: (0, 0)),
            pl.BlockSpec((D, tn), lambda i, j: (0, jnp.minimum(j, last))),
            pl.BlockSpec((D, LANES), lambda i, j: (0, 0)),
            pl.BlockSpec((LANES, kw), lambda i, j: (0, 0)),
            pl.BlockSpec((1, kw), lambda i, j: (0, 0)),
            pl.BlockSpec((tm, HEAD_DIM), lambda i, j: (i % s_tiles, 0)),
            pl.BlockSpec((tm, HEAD_DIM), lambda i, j: (i % s_tiles, 0)),
        ],
        out_specs=[
            pl.BlockSpec((tm, tn), lambda i, j: (i, jnp.minimum(j, last))),
            pl.BlockSpec((tm, kw), lambda i, j: (i, 0)),
        ],
        out_shape=[
            jax.ShapeDtypeStruct((T, n_cols), BF16),
            jax.ShapeDtypeStruct((T, kw), F32),
        ],
        scratch_shapes=[pltpu.VMEM((tm, D), BF16)],
        compiler_params=pltpu.CompilerParams(
            dimension_semantics=("parallel", "arbitrary"), vmem_limit_bytes=VMEM_LIMIT),
        name="in_proj",
    )(x2d, norm_w, w_main, w_gz, w2_pad, gate_b, cosf, sinf)


MOBA_ONES_ROWS = 16


def _moba_select(q_ref, kmean_scr, qs_scr, which, iq, *, nb, blk, scale):
    hd = HEAD_DIM
    qt = q_ref[...].astype(F32).T
    qt_b = qt.astype(BF16)
    km = kmean_scr[...]
    km_hi, km_r = km.astype(BF16), km - km.astype(BF16).astype(F32)
    km_mid, km_lo = _split_bf16(km_r)
    gate = _dot(km_hi, qt_b) + (_dot(km_mid, qt_b) + _dot(km_lo, qt_b))
    n_iota = lax.broadcasted_iota(I32, (nb, blk), 0)
    rank = jnp.zeros((nb, blk), I32)
    for m in range(nb):
        gm = gate[m:m + 1, :]
        beats = (gm > gate) | ((gm == gate) & (m < n_iota))
        rank = rank + jnp.where(beats & (m < iq), 1, 0)
    keep = ((n_iota < iq) & (rank < MOBA_TOPK)) | (n_iota == iq)
    maskv = jnp.where(keep, 0.0, NEG)
    maskv = jnp.concatenate([maskv, jnp.zeros((hd - nb, blk), F32)], axis=0)
    qs_scr[which, :hd, :] = (qt * scale).astype(BF16)
    qs_scr[which, hd:, :] = maskv.astype(BF16)


def _moba_kernel(qa_ref, qb_ref, k_ref, v_ref, oa_ref, ob_ref,
                 kmean_scr, kaug_scr, vt_scr, qs_scr, s_scr, acc_scr, *, nb, blk, scale):
    p = pl.program_id(2)
    hd = HEAD_DIM
    sub = 8

    @pl.when(p == 0)
    def _():
        col = lax.broadcasted_iota(I32, (blk, hd), 1)
        ones = jnp.ones((MOBA_ONES_ROWS, blk), BF16)
        for n in range(nb):
            rows = slice(n * blk, (n + 1) * blk)
            kb = k_ref[rows, :]
            kmean_scr[n:n + 1, :] = jnp.sum(kb.astype(F32), axis=0, keepdims=True) * (1.0 / blk)
            kaug_scr[rows, :hd] = kb
            kaug_scr[rows, hd:] = jnp.where(col == n, 1.0, 0.0).astype(BF16)
            vt_scr[n, :hd, :] = v_ref[rows, :].astype(F32).T.astype(BF16)
            vt_scr[n, hd:, :] = ones

    ia = p
    ib = nb - 1 - p
    _moba_select(qa_ref, kmean_scr, qs_scr, 0, ia, nb=nb, blk=blk, scale=scale)
    _moba_select(qb_ref, kmean_scr, qs_scr, 1, ib, nb=nb, blk=blk, scale=scale)

    def fold_max(x):
        return jnp.max(x.reshape(blk // sub, sub, blk), axis=0)

    def key_rows(jb):
        return kaug_scr[pl.ds(pl.multiple_of(jb * blk, blk), blk), :]

    kpos = lax.broadcasted_iota(I32, (blk, blk), 0)
    qpos = lax.broadcasted_iota(I32, (blk, blk), 1)
    causal = jnp.where(kpos <= qpos, 0.0, NEG)

    s_own_a = _dot(key_rows(ia), qs_scr[0]) + causal
    s_own_b = _dot(key_rows(ib), qs_scr[1]) + causal
    mx_a = fold_max(s_own_a)
    mx_b = fold_max(s_own_b)
    trips = []
    for t in range(nb - 1):
        is_a = t < p
        which = jnp.where(is_a, 0, 1)
        jb = jnp.where(is_a, t, t - p)
        trips.append((is_a, which, jb))
        s = _dot(key_rows(jb), qs_scr[which])
        s_scr[t] = s
        f = fold_max(s)
        mx_a = jnp.maximum(mx_a, jnp.where(is_a, f, NEG))
        mx_b = jnp.maximum(mx_b, jnp.where(is_a, NEG, f))
    m_a = jnp.max(mx_a, axis=0, keepdims=True)
    m_b = jnp.max(mx_b, axis=0, keepdims=True)

    acc_scr[0] = _dot(vt_scr[ia], jnp.exp2(s_own_a - m_a).astype(BF16))
    acc_scr[1] = _dot(vt_scr[ib], jnp.exp2(s_own_b - m_b).astype(BF16))
    for t, (is_a, which, jb) in enumerate(trips):
        pr = jnp.exp2(s_scr[t] - jnp.where(is_a, m_a, m_b))
        da = _dot(vt_scr[jb], pr.astype(BF16))
        acc_scr[0] += jnp.where(is_a, da, 0.0)
        acc_scr[1] += jnp.where(is_a, 0.0, da)
    for which, o_ref in ((0, oa_ref), (1, ob_ref)):
        acc = acc_scr[which]
        o_ref[...] = (acc[:hd] / acc[hd:hd + 1]).T.astype(BF16)


def _moba(proj, batch, seq):
    blk = MOBA_BLOCK
    nb = seq // blk
    assert nb % 2 == 0 and nb <= HEAD_DIM
    half = nb // 2
    H = MOBA_HEADS
    hd = HEAD_DIM
    width = H * hd
    lo, hi = pl.pallas_call(
        functools.partial(_moba_kernel, nb=nb, blk=blk, scale=hd ** -0.5 * LOG2_E),
        grid=(batch, H, half),
        in_specs=[
            pl.BlockSpec((blk, hd), lambda b, h, p: (b * nb + p, h)),
            pl.BlockSpec((blk, hd), lambda b, h, p: (b * nb + nb - 1 - p, h)),
            pl.BlockSpec((seq, hd), lambda b, h, p: (b, H + h)),
            pl.BlockSpec((seq, hd), lambda b, h, p: (b, 2 * H + h)),
        ],
        out_specs=[
            pl.BlockSpec((blk, hd), lambda b, h, p: (b * half + p, h)),
            pl.BlockSpec((blk, hd), lambda b, h, p: (b * half + half - 1 - p, h)),
        ],
        out_shape=[jax.ShapeDtypeStruct((batch * half * blk, width), BF16)] * 2,
        scratch_shapes=[
            pltpu.VMEM((nb, hd), F32),
            pltpu.VMEM((seq, 2 * hd), BF16),
            pltpu.VMEM((nb, hd + MOBA_ONES_ROWS, blk), BF16),
            pltpu.VMEM((2, 2 * hd, blk), BF16),
            pltpu.VMEM((nb - 1, blk, blk), F32),
            pltpu.VMEM((2, hd + MOBA_ONES_ROWS, blk), F32),
        ],
        compiler_params=pltpu.CompilerParams(
            dimension_semantics=("parallel", "parallel", "arbitrary"), vmem_limit_bytes=VMEM_LIMIT),
        name="moba",
    )(proj, proj, proj, proj)
    return lo, hi


def _gla_levels(c):
    out = []
    m = c // 2
    while m >= 1:
        out.append(m)
        m //= 2
    return out


def _gla_kernel(q_ref, k_ref, v_ref, g_ref, la_ref, nw_ref, o_ref, st_scr, *,
                chunk, group, heads, scale):
    C = chunk
    levels = _gla_levels(C)

    @pl.when(pl.program_id(2) == 0)
    def _():
        st_scr[...] = jnp.zeros_like(st_scr)

    t_io = lax.broadcasted_iota(I32, (C, C), 0)
    s_io = lax.broadcasted_iota(I32, (C, C), 1)
    sel_rows = [jnp.where(s_io <= t_io, 1.0, 0.0)]
    masks = []
    for m in levels:
        blk_t = t_io // (2 * m)
        bound = blk_t * (2 * m) + (m - 1)
        sel_rows.append(jnp.where(s_io <= bound, 1.0, 0.0))
        masks.append((blk_t == s_io // (2 * m)) & (t_io % (2 * m) >= m) & (s_io % (2 * m) < m))
    cum_mat = jnp.concatenate(sel_rows, axis=0).astype(BF16)
    diag = t_io == s_io
    nw = nw_ref[...]

    dk, dv = GLA_DK, GLA_DV
    for c in range(group):
        rows = slice(c * C, (c + 1) * C)
        for hh in range(heads):
            kc = slice(hh * dk, (hh + 1) * dk)
            vc = slice(hh * dv, (hh + 1) * dv)
            la_hi, la_lo = _split_bf16(la_ref[rows, kc])
            cums = _dot(cum_mat, la_hi) + _dot(cum_mat, la_lo)
            b = cums[:C]
            q = q_ref[rows, kc].astype(F32) * scale
            k = k_ref[rows, kc].astype(F32)
            v = v_ref[rows, vc]
            a = jnp.where(diag, _dot_nt(q.astype(BF16), k.astype(BF16)), 0.0)
            for li in range(len(levels)):
                e = jnp.exp(-jnp.abs(b - cums[(li + 1) * C:(li + 2) * C]))
                a = a + jnp.where(masks[li], _dot_nt((q * e).astype(BF16), (k * e).astype(BF16)), 0.0)
            st = st_scr[hh]
            o = _dot(a.astype(BF16), v) + _dot_nt((q * jnp.exp(b)).astype(BF16), st.astype(BF16))
            b_last = b[C - 1:C, :]
            ke = (k * jnp.exp(b_last - b)).astype(BF16)
            st_scr[hh] = st * jnp.exp(b_last) + _dot_tn(v, ke)
            gg = g_ref[rows, vc].astype(F32)
            o_ref[rows, vc] = (_rms(o, nw) * (gg * jax.nn.sigmoid(gg))).astype(BF16)


def _gla(proj, la, norm_w, batch, seq):
    T = proj.shape[0]
    Hg = GLA_HEADS
    rows = GLA_CHUNK * GLA_GROUP
    ng = seq // rows
    hs = GLA_HEADS_PER_STEP
    wk, wv = hs * GLA_DK, hs * GLA_DV
    q0 = (3 * MOBA_HEADS * HEAD_DIM) // wk
    k0 = q0 + Hg // hs
    v0 = (3 * MOBA_HEADS * HEAD_DIM + 2 * Hg * GLA_DK) // wv
    g0 = v0 + Hg // hs
    return pl.pallas_call(
        functools.partial(_gla_kernel, chunk=GLA_CHUNK, group=GLA_GROUP, heads=hs,
                          scale=GLA_DK ** -0.5),
        grid=(batch, Hg // hs, ng),
        in_specs=[
            pl.BlockSpec((rows, wk), lambda b, h, g: (b * ng + g, q0 + h)),
            pl.BlockSpec((rows, wk), lambda b, h, g: (b * ng + g, k0 + h)),
            pl.BlockSpec((rows, wv), lambda b, h, g: (b * ng + g, v0 + h)),
            pl.BlockSpec((rows, wv), lambda b, h, g: (b * ng + g, g0 + h)),
            pl.BlockSpec((rows, wk), lambda b, h, g: (b * ng + g, h)),
            pl.BlockSpec((1, GLA_DV), lambda b, h, g: (0, 0)),
        ],
        out_specs=pl.BlockSpec((rows, wv), lambda b, h, g: (b * ng + g, h)),
        out_shape=jax.ShapeDtypeStruct((T, Hg * GLA_DV), BF16),
        scratch_shapes=[pltpu.VMEM((hs, GLA_DV, GLA_DK), F32)],
        compiler_params=pltpu.CompilerParams(
            dimension_semantics=("parallel", "parallel", "arbitrary"), vmem_limit_bytes=VMEM_LIMIT),
        name="gla",
    )(proj, proj, proj, proj, la, norm_w)


def _split_bf16(x):
    hi = x.astype(BF16)
    return hi, (x - hi.astype(F32)).astype(BF16)


def _pack_bf16_pairs(x):
    n = x.shape[1] // 2
    lo = lax.bitcast_convert_type(x[:, :n].astype(BF16).astype(F32), U32)
    hi = lax.bitcast_convert_type(x[:, n:].astype(BF16).astype(F32), U32)
    return (lo >> 16) | (hi & jnp.uint32(0xFFFF0000))


def _unpack_bf16_pairs(p):
    lo = lax.bitcast_convert_type(p << 16, F32).astype(BF16)
    hi = lax.bitcast_convert_type(p & jnp.uint32(0xFFFF0000), F32).astype(BF16)
    return lo, hi


ROW_TILE = 8


def _token_tile(ref, t):
    return ref.at[pl.ds(pl.multiple_of(t * ROW_TILE, ROW_TILE), ROW_TILE), :]


def _store_token_rows(ref, x):
    n = x.shape[0]
    for c in range(ROW_TILE):
        ref[pl.ds(c, n, stride=ROW_TILE), :] = x[:, c * LANES:(c + 1) * LANES]


def _load_token_rows(ref, t0, n):
    base = pl.multiple_of(t0 * ROW_TILE, ROW_TILE)
    return jnp.concatenate(
        [ref[pl.ds(base + c, n, stride=ROW_TILE), :] for c in range(ROW_TILE)], axis=1)


def _out_proj_kernel(mlo_ref, mhi_ref, gl_ref, x_ref, wo_ref, nw_ref, rwh_ref, rwl_ref, rb_ref,
                     x1_ref, h2p_ref, idx_ref, gate_ref, rank_ref, cnt_ref, mo_scr, cnt_scr, *, seq_tiles):
    wm = mlo_ref.shape[1]
    first_half = (pl.program_id(0) % seq_tiles) < seq_tiles // 2
    @pl.when(first_half)
    def _():
        mo_scr[...] = mlo_ref[...]

    @pl.when(jnp.logical_not(first_half))
    def _():
        mo_scr[...] = mhi_ref[...]

    x1 = x_ref[...] + _dot(mo_scr[...], wo_ref[:wm, :]) + _dot(gl_ref[...], wo_ref[wm:, :])
    x1_ref[...] = x1
    h2 = _rms(x1, nw_ref[...])
    _store_token_rows(h2p_ref, _pack_bf16_pairs(h2))
    h_hi, h_lo = _split_bf16(h2)
    rw_hi = rwh_ref[...]
    logits = _dot_nt(rw_hi, h_hi) + (_dot_nt(rwl_ref[...], h_hi) + _dot_nt(rw_hi, h_lo))
    tm = logits.shape[1]
    logits = logits + jnp.concatenate([rb_ref[...]] * (tm // LANES), axis=1)
    row = lax.broadcasted_iota(I32, logits.shape, 0)
    vals, idxs = [], []
    for _ in range(TOP_K):
        mx = jnp.max(logits, axis=0, keepdims=True)
        ix = jnp.min(jnp.where(logits == mx, row, LANES), axis=0, keepdims=True)
        vals.append(mx)
        idxs.append(ix)
        logits = jnp.where(row == ix, -jnp.inf, logits)
    exps = [jnp.exp(v - vals[0]) for v in vals]
    denom = exps[0]
    for e in exps[1:]:
        denom = denom + e
    idx_t = jnp.zeros(logits.shape, I32)
    gate_t = jnp.zeros(logits.shape, F32)
    for r in range(TOP_K):
        idx_t = jnp.where(row == r, idxs[r], idx_t)
        gate_t = jnp.where(row == r, exps[r] / denom, gate_t)
    idx_ref[...] = idx_t[:idx_ref.shape[0], :]
    gate_ref[...] = gate_t.T

    @pl.when(pl.program_id(0) == 0)
    def _():
        cnt_scr[...] = jnp.zeros_like(cnt_scr)

    picks = [row == idxs[r] for r in range(TOP_K)]
    picked = picks[0]
    for pk in picks[1:]:
        picked = picked | pk
    cmat = jnp.where(picked, 1.0, 0.0).astype(BF16)
    t_r = lax.broadcasted_iota(I32, (tm, tm), 0)
    t_c = lax.broadcasted_iota(I32, (tm, tm), 1)
    earlier = jnp.where(t_r < t_c, 1.0, 0.0).astype(BF16)
    base = cnt_scr[...]
    before = _dot(cmat, earlier) + jnp.concatenate([base] * (tm // LANES), axis=1)
    rank_t = jnp.zeros(logits.shape, F32)
    for r in range(TOP_K):
        rank_t = jnp.where(row == r, jnp.sum(jnp.where(picks[r], before, 0.0), axis=0, keepdims=True), rank_t)
    rank_ref[...] = rank_t[:rank_ref.shape[0], :].astype(I32)
    cnt_scr[...] = base + _dot(cmat, jnp.ones((tm, LANES), BF16))
    cnt_ref[...] = cnt_scr[...]


def _out_proj(moba_lo, moba_hi, gla_o, x2d, w_out, norm_w, rw_hi, rw_lo, rb_tile, seq):
    T, D = x2d.shape
    tm = PROJ_TM
    wm, wg = moba_lo.shape[1], gla_o.shape[1]
    st = seq // tm
    ht = st // 2
    assert st % 2 == 0

    def lo_map(i):
        return ((i // st) * ht + jnp.minimum(i % st, ht - 1), 0)

    def hi_map(i):
        return ((i // st) * ht + jnp.maximum(i % st - ht, 0), 0)

    return pl.pallas_call(
        functools.partial(_out_proj_kernel, seq_tiles=st),
        grid=(T // tm,),
        in_specs=[
            pl.BlockSpec((tm, wm), lo_map),
            pl.BlockSpec((tm, wm), hi_map),
            pl.BlockSpec((tm, wg), lambda i: (i, 0)),
            pl.BlockSpec((tm, D), lambda i: (i, 0)),
            pl.BlockSpec((wm + wg, D), lambda i: (0, 0)),
            pl.BlockSpec((1, D), lambda i: (0, 0)),
            pl.BlockSpec((LANES, D), lambda i: (0, 0)),
            pl.BlockSpec((LANES, D), lambda i: (0, 0)),
            pl.BlockSpec((LANES, LANES), lambda i: (0, 0)),
        ],
        out_specs=[
            pl.BlockSpec((tm, D), lambda i: (i, 0)),
            pl.BlockSpec((tm * ROW_TILE, LANES), lambda i: (i, 0)),
            pl.BlockSpec((ROW_TILE, tm), lambda i: (0, i)),
            pl.BlockSpec((tm, LANES), lambda i: (i, 0)),
            pl.BlockSpec((ROW_TILE, tm), lambda i: (0, i)),
            pl.BlockSpec((LANES, LANES), lambda i: (0, 0)),
        ],
        out_shape=[
            jax.ShapeDtypeStruct((T, D), F32),
            jax.ShapeDtypeStruct((T * ROW_TILE, LANES), U32),
            jax.ShapeDtypeStruct((ROW_TILE, T), I32),
            jax.ShapeDtypeStruct((T, LANES), F32),
            jax.ShapeDtypeStruct((ROW_TILE, T), I32),
            jax.ShapeDtypeStruct((LANES, LANES), F32),
        ],
        scratch_shapes=[pltpu.VMEM((tm, wm), BF16), pltpu.VMEM((LANES, LANES), F32)],
        compiler_params=pltpu.CompilerParams(
            dimension_semantics=("arbitrary",), vmem_limit_bytes=VMEM_LIMIT),
        name="out_proj",
    )(moba_lo, moba_hi, gla_o, x2d, w_out, norm_w, rw_hi, rw_lo, rb_tile)


def _slot_bits(n_rows):
    return (n_rows - 1).bit_length()


def _moe_row_in(h_hbm, xbuf, sem, slot, tok, r):
    return pltpu.make_async_copy(_token_tile(h_hbm, tok), _token_tile(xbuf.at[slot], r), sem.at[slot])


def _moe_row_out(acc, y_hbm, sem, r, dst_row):
    return pltpu.make_async_copy(acc.at[pl.ds(r, 1), :], y_hbm.at[pl.ds(dst_row, 1), :], sem)


def _moe_blk_out(acc, y_hbm, sem, blk):
    dst = y_hbm.at[pl.ds(pl.multiple_of(blk * MOE_SUB, MOE_SUB), MOE_SUB), :]
    return pltpu.make_async_copy(acc.at[pl.ds(0, MOE_SUB), :], dst, sem)


def _moe_kernel(se_ref, rs_ref, ns_ref, asg_ref,
                h_hbm, wg_ref, wu_ref, wd_ref, bg_ref, bu_ref, bd_ref, y_hbm,
                xbuf, acc, gsem, osem, *, slot_bits, nf):
    s = pl.program_id(0)
    j = pl.program_id(1)
    n_grid = pl.num_programs(0)
    nsub = ns_ref[s]
    base = rs_ref[s] * MOE_SUB
    slot = s & 1
    per_trip = MOE_SUB // nf
    half = wg_ref.shape[1] // 2

    def src_token(p):
        return lax.shift_right_logical(asg_ref[p], slot_bits)

    def out_slot(p):
        return asg_ref[p] & ((1 << slot_bits) - 1)

    def wait_rows(n_sub_tiles):
        def wait(gidx, c):
            for u in range(MOE_DMA_UNROLL):
                _moe_row_in(h_hbm, xbuf, gsem, slot, 0, gidx * MOE_DMA_UNROLL + u).wait()
            return c

        lax.fori_loop(0, n_sub_tiles * (MOE_SUB // MOE_DMA_UNROLL), wait, 0)

    @pl.when(j == 0)
    def _():
        @pl.when(s == 0)
        def _():
            def issue(gidx, c):
                for u in range(MOE_DMA_UNROLL):
                    r = gidx * MOE_DMA_UNROLL + u
                    _moe_row_in(h_hbm, xbuf, gsem, slot, src_token(base + r), r).start()
                return c

            lax.fori_loop(0, nsub * (MOE_SUB // MOE_DMA_UNROLL), issue, 0)

        wait_rows(jnp.where(s == 0, nsub, ns_ref[jnp.maximum(s - 1, 0)]))

    @pl.when(nsub > 0)
    def _():
        @pl.when(j == 0)
        def _():
            bias_rows = jnp.broadcast_to(bd_ref[0], (MOE_SUB, acc.shape[1]))

            def init(r, c):
                acc[pl.ds(pl.multiple_of(r * MOE_SUB, MOE_SUB), MOE_SUB), :] = bias_rows
                return c

            lax.fori_loop(0, nsub, init, 0)

        bg = bg_ref[0]
        bu = bu_ref[0]
        next_base = rs_ref[s + 1] * MOE_SUB

        def tile(r, n, emit):
            rows = n * MOE_SUB
            off = pl.multiple_of(r * MOE_SUB, MOE_SUB)
            xl, xh = _unpack_bf16_pairs(_load_token_rows(xbuf.at[slot], off, rows))
            g = (_dot(xl, wg_ref[0, :half, :].astype(BF16))
                 + _dot(xh, wg_ref[0, half:, :].astype(BF16)) + bg)
            u = (_dot(xl, wu_ref[0, :half, :].astype(BF16))
                 + _dot(xh, wu_ref[0, half:, :].astype(BF16)) + bu)
            g = jnp.minimum(g, SWIGLU_LIMIT)
            u = jnp.clip(u, -SWIGLU_LIMIT, SWIGLU_LIMIT)
            act = (u + 1.0) * (g * jax.nn.sigmoid(SWIGLU_ALPHA * g))
            acc[pl.ds(off, rows), :] += _dot(act.astype(BF16), wd_ref[0].astype(BF16))
            n_in = n * per_trip
            g0 = pl.multiple_of((j * nsub + r) * per_trip, per_trip)
            x_next = xbuf.at[1 - slot, pl.ds(g0 * ROW_TILE, n_in * ROW_TILE), :]
            for v in range(n_in):
                tok = src_token(next_base + g0 + v)
                pltpu.make_async_copy(_token_tile(h_hbm, tok), _token_tile(x_next, v),
                                      gsem.at[1 - slot]).start()
            if emit:
                acc_rows = acc.at[pl.ds(off, rows), :]
                for v in range(rows):
                    _moe_row_out(acc_rows, y_hbm, osem, v, out_slot(base + off + v)).start(priority=v % 2)

        def sweep(emit):
            def pair(t, c):
                tile(2 * t, 2, emit)
                return c

            lax.fori_loop(0, nsub >> 1, pair, 0)

            @pl.when((nsub & 1) == 1)
            def _():
                tile(nsub - 1, 1, emit)

        @pl.when(j < nf - 1)
        def _():
            sweep(False)

        @pl.when(j == nf - 1)
        def _():
            sweep(True)

            def done(gidx, c):
                for u in range(MOE_DMA_UNROLL):
                    _moe_row_out(acc, y_hbm, osem, 0, 0).wait()
                return c

            lax.fori_loop(0, nsub * (MOE_SUB // MOE_DMA_UNROLL), done, 0)

    @pl.when((s == n_grid - 1) & (j == nf - 1))
    def _():
        tail0 = rs_ref[n_grid]
        n_blk = y_hbm.shape[0] // MOE_SUB
        acc[0:MOE_SUB, :] = jnp.zeros((MOE_SUB, acc.shape[1]), F32)

        def put(b, c):
            _moe_blk_out(acc, y_hbm, osem, b).start()
            return c

        lax.fori_loop(tail0, n_blk, put, 0)

        def done(b, c):
            _moe_blk_out(acc, y_hbm, osem, b).wait()
            return c

        lax.fori_loop(tail0, n_blk, done, 0)


def _moe(sup_e, sup_blk, sup_nsub, row_asg, h2p, w_gu, b_gu, w_dn, b_dn, n_rows):
    T = h2p.shape[0] // ROW_TILE
    D = 2 * ROW_TILE * LANES
    assert w_gu.shape[1] == D
    E, _, F2 = w_gu.shape
    F = F2 // 2
    tf = MOE_TF
    nf = F // tf
    assert MOE_SUB % nf == 0
    n_sup = sup_e.shape[0]

    def wmap(col0):
        def f(s, j, se, rs, ns, tok):
            return (se[s], 0, col0 + jnp.where(ns[s] > 0, j, nf - 1))
        return f

    def dmap(s, j, se, rs, ns, tok):
        return (se[s], jnp.where(ns[s] > 0, j, nf - 1), 0)

    def bdmap(s, j, se, rs, ns, tok):
        return (se[s], 0, 0)

    grid_spec = pltpu.PrefetchScalarGridSpec(
        num_scalar_prefetch=4,
        grid=(n_sup, nf),
        in_specs=[
            pl.BlockSpec(memory_space=pl.ANY),
            pl.BlockSpec((1, D, tf), wmap(0)),
            pl.BlockSpec((1, D, tf), wmap(nf)),
            pl.BlockSpec((1, tf, D), dmap),
            pl.BlockSpec((1, 1, tf), wmap(0)),
            pl.BlockSpec((1, 1, tf), wmap(nf)),
            pl.BlockSpec((1, 1, D), bdmap),
        ],
        out_specs=pl.BlockSpec(memory_space=pl.ANY),
        scratch_shapes=[
            pltpu.VMEM((2, MOE_SUP * ROW_TILE, LANES), U32),
            pltpu.VMEM((MOE_SUP, D), F32),
            pltpu.SemaphoreType.DMA((2,)),
            pltpu.SemaphoreType.DMA,
        ],
    )
    return pl.pallas_call(
        functools.partial(_moe_kernel, slot_bits=_slot_bits(n_rows), nf=nf),
        grid_spec=grid_spec,
        out_shape=jax.ShapeDtypeStruct((n_rows, D), F32),
        compiler_params=pltpu.CompilerParams(
            dimension_semantics=("arbitrary", "arbitrary"), vmem_limit_bytes=VMEM_LIMIT),
        name="moe",
    )(sup_e, sup_blk, sup_nsub, row_asg, h2p, w_gu, w_gu, w_dn,
      b_gu.reshape(E, 1, F2), b_gu.reshape(E, 1, F2), b_dn.reshape(E, 1, D))


def _combine_kernel(*refs):
    y_refs = refs[:TOP_K]
    x1_ref, gate_ref, nw_ref, o_ref = refs[TOP_K:]
    gates = gate_ref[...]
    x2 = x1_ref[...]
    for kk in range(TOP_K):
        x2 = x2 + gates[:, kk:kk + 1] * y_refs[kk][...]
    o_ref[...] = _rms(x2, nw_ref[...])


def _combine(y_slots, x1, gates, norm_w):
    T, D = x1.shape
    tm = COMB_TM
    tiles = T // tm

    def plane(kk):
        return pl.BlockSpec((tm, D), lambda i: (kk * tiles + i, 0))

    return pl.pallas_call(
        _combine_kernel,
        grid=(tiles,),
        in_specs=[plane(kk) for kk in range(TOP_K)] + [
            pl.BlockSpec((tm, D), lambda i: (i, 0)),
            pl.BlockSpec((tm, LANES), lambda i: (i, 0)),
            pl.BlockSpec((1, D), lambda i: (0, 0)),
        ],
        out_specs=pl.BlockSpec((tm, D), lambda i: (i, 0)),
        out_shape=jax.ShapeDtypeStruct((T, D), F32),
        compiler_params=pltpu.CompilerParams(
            dimension_semantics=("parallel",), vmem_limit_bytes=VMEM_LIMIT),
        name="combine",
    )(*([y_slots] * TOP_K), x1, gates, norm_w)


def _route(top_idx_t, rank_t, counts, n_experts):
    T = top_idx_t.shape[1]
    A = T * TOP_K
    G, SUP = MOE_SUB, MOE_SUP
    flat_e = top_idx_t.reshape(A)
    rank = rank_t.reshape(A)
    pcounts = (counts + G - 1) // G * G
    pends = jnp.cumsum(pcounts)
    pstarts = pends - pcounts
    dest = (pstarts[flat_e] + rank).astype(I32)
    n_rows = (A + n_experts * (G - 1)) // G * G
    p_ids = jnp.arange(n_rows, dtype=I32)
    seg = jnp.minimum(jnp.sum((p_ids[:, None] >= pends[None, :]).astype(I32), axis=1), n_experts - 1)
    pad_ids = A + p_ids - jnp.cumsum(counts)[seg]
    a_ids = jnp.arange(A, dtype=I32)
    bits = _slot_bits(n_rows)
    assert (T - 1) << bits < 2 ** 31
    row_asg = ((T - 1) << bits | pad_ids).astype(I32).at[dest].set((a_ids % T) << bits | a_ids)
    row_asg = jnp.concatenate([row_asg, jnp.zeros((SUP,), I32)])

    n_sup = (A + n_experts * (G - 1) + n_experts * (SUP - G)) // SUP + 1
    sub_e = pcounts // G
    nsup_e = (pcounts + SUP - 1) // SUP
    sup_ends = jnp.cumsum(nsup_e)
    total = sup_ends[-1]
    s_ids = jnp.arange(n_sup, dtype=I32)
    s_eff = jnp.minimum(s_ids, total - 1)
    e_of = jnp.sum((s_eff[:, None] >= sup_ends[None, :]).astype(I32), axis=1)
    local = s_eff - (sup_ends - nsup_e)[e_of]
    active = s_ids < total
    n_e = jnp.maximum(nsup_e[e_of], 1)
    small, extra = sub_e[e_of] // n_e, sub_e[e_of] % n_e
    blk0 = jnp.where(active, pstarts[e_of] // G + local * small + jnp.minimum(local, extra), 0)
    nsub = jnp.where(active, small + (local < extra).astype(I32), 0)
    order = jnp.argsort(-nsub, stable=True)
    e_sorted = e_of[order]
    sup_e = jnp.where(active, e_sorted, e_sorted[total - 1]).astype(I32)
    sup_blk = jnp.concatenate([blk0[order], pends[-1:] // G]).astype(I32)
    sup_nsub = nsub[order].astype(I32)
    return row_asg, n_rows, sup_e, sup_blk, sup_nsub


def _rope_tables(seq):
    half = HEAD_DIM // 2
    inv_freq = 1.0 / (ROPE_THETA ** (jnp.arange(0, HEAD_DIM, 2, dtype=F32) / HEAD_DIM))
    ang = jnp.arange(seq, dtype=F32)[:, None] * inv_freq[None, :]
    cos, sin = jnp.cos(ang), jnp.sin(ang)
    assert cos.shape[1] == half
    return jnp.concatenate([cos, cos], axis=1), jnp.concatenate([-sin, sin], axis=1)


def kernel(x, attn_norm_w, w_in, gla_gate_w2, gla_gate_b, gla_norm_w, w_out, ffn_norm_w,
           router_w, router_b, w_gate_up, b_gate_up, w_down, b_down, final_norm_w):
    B, S, D = x.shape
    T = B * S
    assert w_in.shape[0] == 1, "single-layer stack: the final norm is fused into combine"
    n_experts = router_w.shape[-1]
    n_main = w_in.shape[-1] - GLA_GATE_RANK
    cosf, sinf = _rope_tables(S)
    x2d = x.reshape(T, D)
    w_main = w_in[0, :, :n_main].astype(BF16)
    w_gz = jnp.pad(w_in[0, :, n_main:], ((0, 0), (0, LANES - GLA_GATE_RANK))).astype(BF16)
    w2_pad = jnp.pad(gla_gate_w2[0], ((0, LANES - GLA_GATE_RANK), (0, 0)))
    proj, log_a = _in_proj(x2d, attn_norm_w[0][None, :], w_main, w_gz, w2_pad,
                           gla_gate_b[0][None, :], cosf, sinf, S)
    moba_lo, moba_hi = _moba(proj, B, S)
    gla_o = _gla(proj, log_a, gla_norm_w[0][None, :], B, S)
    rw_t = jnp.pad(router_w[0].T, ((0, LANES - n_experts), (0, 0)))
    rw_hi = rw_t.astype(BF16)
    rw_lo = (rw_t - rw_hi.astype(F32)).astype(BF16)
    rb_col = jnp.pad(router_b[0], (0, LANES - n_experts), constant_values=NEG)
    rb_tile = jnp.broadcast_to(rb_col[:, None], (LANES, LANES))
    x1, h2p, top_idx_t, gates, rank_t, counts = _out_proj(
        moba_lo, moba_hi, gla_o, x2d, w_out[0].astype(BF16), ffn_norm_w[0][None, :], rw_hi, rw_lo, rb_tile, S)
    row_asg, n_rows, sup_e, sup_blk, sup_nsub = _route(
        top_idx_t[:TOP_K], rank_t[:TOP_K], counts[:n_experts, 0].astype(I32), n_experts)
    y_slots = _moe(sup_e, sup_blk, sup_nsub, row_asg, h2p, w_gate_up[0], b_gate_up[0],
                   w_down[0], b_down[0], n_rows)
    out = _combine(y_slots, x1, gates, final_norm_w[None, :])
    return out.reshape(B, S, D)
```
